```python
import jax, jax.numpy as jnp
from jax import lax
import numpy as np

D_MODEL = 2048
BATCH = 8
SEQ = 2048
DEPTH = 1

MEM_LEN = 256
HEAD_DIM = 128
CONV_WIDTH = D_MODEL // 2
ATTN_WIDTH = D_MODEL - CONV_WIDTH
N_ATTN_HEADS = ATTN_WIDTH // HEAD_DIM
CONV_KERNEL = 31
MOBA_BLOCK = 256
MOBA_TOPK = 3
MOBA_Q_CHUNK = 16
N_XATTN_HEADS = 4
XATTN_HEAD_DIM = 128
XATTN_WIDTH = N_XATTN_HEADS * XATTN_HEAD_DIM
D_FF = 5632
ROPE_THETA = 10000.0
RMS_EPS = 1e-6
LN_EPS = 1e-5
FFN_RES_SCALE = 0.5
IN_PROJ_WIDTH = 2 * CONV_WIDTH + 3 * ATTN_WIDTH
NEG_INF = -1e30

kernel_name = 'hybrid_conformer_moba_macaron_layer'


def rms_norm(x, g):
    xf = x.astype(jnp.float32)
    y = xf * lax.rsqrt(jnp.mean(xf * xf, axis=-1, keepdims=True) + RMS_EPS)
    return (y * g.astype(jnp.float32)).astype(x.dtype)


def layer_norm(x, g, b):
    xf = x.astype(jnp.float32)
    mu = jnp.mean(xf, axis=-1, keepdims=True)
    xc = xf - mu
    var = jnp.mean(xc * xc, axis=-1, keepdims=True)
    y = xc * lax.rsqrt(var + LN_EPS) * g.astype(jnp.float32) + b.astype(jnp.float32)
    return y.astype(x.dtype)


def swiglu_ffn(x, w_gu, w_down):
    gate, up = jnp.split(x @ w_gu, 2, axis=-1)
    return (jax.nn.silu(gate) * up) @ w_down


def rope_tables(seq, dim):
    inv_freq = 1.0 / (ROPE_THETA ** (jnp.arange(0, dim, 2, dtype=jnp.float32) / dim))
    ang = jnp.arange(seq, dtype=jnp.float32)[:, None] * inv_freq[None, :]
    return jnp.cos(ang), jnp.sin(ang)


def apply_rope(x, cos, sin):
    x1, x2 = jnp.split(x.astype(jnp.float32), 2, axis=-1)
    c, s = cos[None, None], sin[None, None]
    return jnp.concatenate([x1 * c - x2 * s, x2 * c + x1 * s], axis=-1).astype(x.dtype)


def conformer_conv_group(a, g, w_dw, b_dw, ln_g, ln_b):
    y = a * jax.nn.sigmoid(g)
    y = lax.conv_general_dilated(
        y, w_dw[:, None, :].astype(y.dtype), window_strides=(1,),
        padding=[(CONV_KERNEL - 1, 0)],
        dimension_numbers=('NWC', 'WIO', 'NWC'),
        feature_group_count=CONV_WIDTH) + b_dw
    y = layer_norm(y, ln_g, ln_b)
    return jax.nn.silu(y)


def moba_attention(q, k, v):
    B_, H_, S_, D_ = q.shape
    n_blocks = -(-S_ // MOBA_BLOCK)
    s_pad = n_blocks * MOBA_BLOCK
    pad = [(0, 0), (0, 0), (0, s_pad - S_), (0, 0)]
    q, k, v = jnp.pad(q, pad), jnp.pad(k, pad), jnp.pad(v, pad)
    kb = k.reshape(B_, H_, n_blocks, MOBA_BLOCK, D_)
    vb = v.reshape(B_, H_, n_blocks, MOBA_BLOCK, D_)
    k_mean = jnp.mean(kb.astype(jnp.float32), axis=3)
    top_k = min(MOBA_TOPK, n_blocks)
    scale = D_ ** -0.5
    n_chunks = s_pad // MOBA_Q_CHUNK
    q_chunks = q.reshape(B_, H_, n_chunks, MOBA_Q_CHUNK, D_).transpose(2, 0, 1, 3, 4)
    b_idx = jnp.arange(B_)[:, None, None, None]
    h_idx = jnp.arange(H_)[None, :, None, None]
    blk_ids = jnp.arange(n_blocks)
    sel_len = top_k * MOBA_BLOCK

    def one_chunk(args):
        c, q_c = args
        q_pos = c * MOBA_Q_CHUNK + jnp.arange(MOBA_Q_CHUNK)
        own = (c * MOBA_Q_CHUNK) // MOBA_BLOCK
        qf = q_c.astype(jnp.float32)
        gate = jnp.einsum('bhqd,bhnd->bhqn', qf, k_mean)
        gate = jnp.where(blk_ids < own, gate, NEG_INF)
        _, idx = lax.top_k(gate, top_k)
        valid = jnp.arange(top_k) < own
        k_sel = kb[b_idx, h_idx, idx].astype(jnp.float32)
        v_sel = vb[b_idx, h_idx, idx].astype(jnp.float32)
        s_sel = jnp.einsum('bhqd,bhqknd->bhqkn', qf, k_sel) * scale
        s_sel = jnp.where(valid[:, None], s_sel, NEG_INF).reshape(B_, H_, MOBA_Q_CHUNK, sel_len)
        k_own = lax.dynamic_index_in_dim(kb, own, axis=2, keepdims=False).astype(jnp.float32)
        v_own = lax.dynamic_index_in_dim(vb, own, axis=2, keepdims=False).astype(jnp.float32)
        s_own = jnp.einsum('bhqd,bhnd->bhqn', qf, k_own) * scale
        k_pos = own * MOBA_BLOCK + jnp.arange(MOBA_BLOCK)
        s_own = jnp.where(k_pos[None, :] <= q_pos[:, None], s_own, NEG_INF)
        p = jax.nn.softmax(jnp.concatenate([s_sel, s_own], axis=-1), axis=-1)
        p_sel = p[..., :sel_len].reshape(B_, H_, MOBA_Q_CHUNK, top_k, MOBA_BLOCK)
        out = (jnp.einsum('bhqkn,bhqknd->bhqd', p_sel, v_sel)
               + jnp.einsum('bhqn,bhnd->bhqd', p[..., sel_len:], v_own))
        return out.astype(q_c.dtype)

    out = lax.map(one_chunk, (jnp.arange(n_chunks), q_chunks))
    out = out.transpose(1, 2, 0, 3, 4).reshape(B_, H_, s_pad, D_)
    return out[:, :, :S_]


def memory_cross_attention(u, mem_n, w_q, w_kv, w_o):
    B_, S_, _ = u.shape
    M_ = mem_n.shape[1]
    q = (u @ w_q).reshape(B_, S_, N_XATTN_HEADS, XATTN_HEAD_DIM)
    k, v = jnp.split(mem_n @ w_kv, 2, axis=-1)
    k = k.reshape(B_, M_, N_XATTN_HEADS, XATTN_HEAD_DIM)
    v = v.reshape(B_, M_, N_XATTN_HEADS, XATTN_HEAD_DIM)
    s = jnp.einsum('bshd,bmhd->bhsm', q.astype(jnp.float32), k.astype(jnp.float32)) * (XATTN_HEAD_DIM ** -0.5)
    p = jax.nn.softmax(s, axis=-1)
    o = jnp.einsum('bhsm,bmhd->bshd', p, v.astype(jnp.float32)).astype(u.dtype)
    return o.reshape(B_, S_, XATTN_WIDTH) @ w_o


def setup_inputs(seed: int = 0) -> dict:
    key = jax.random.key(seed)
    ks = jax.random.split(key, 26)
    L = DEPTH

    def w(k, shape, fan_in):
        return jax.random.normal(k, shape, jnp.float32) * (fan_in ** -0.5)

    def gain(k, n):
        return 1.0 + 0.05 * jax.random.normal(k, (L, n), jnp.float32)

    def small(k, shape):
        return 0.02 * jax.random.normal(k, shape, jnp.float32)

    return {
        'x': jax.random.normal(ks[0], (BATCH, SEQ, D_MODEL), jnp.float32),
        'mem': jax.random.normal(ks[1], (BATCH, MEM_LEN, D_MODEL), jnp.float32),
        'ffn1_pre_g': gain(ks[2], D_MODEL),
        'ffn1_w_gu': w(ks[3], (L, D_MODEL, 2 * D_FF), D_MODEL),
        'ffn1_w_down': w(ks[4], (L, D_FF, D_MODEL), D_FF),
        'ffn1_post_g': gain(ks[5], D_MODEL),
        'mix_pre_g': gain(ks[6], D_MODEL),
        'w_in': w(ks[7], (L, D_MODEL, IN_PROJ_WIDTH), D_MODEL),
        'conv_w_dw': w(ks[8], (L, CONV_KERNEL, CONV_WIDTH), CONV_KERNEL),
        'conv_b_dw': small(ks[9], (L, CONV_WIDTH)),
        'conv_ln_g': gain(ks[10], CONV_WIDTH),
        'conv_ln_b': small(ks[11], (L, CONV_WIDTH)),
        'w_out': w(ks[12], (L, CONV_WIDTH + ATTN_WIDTH, D_MODEL), CONV_WIDTH + ATTN_WIDTH),
        'mix_post_g': gain(ks[13], D_MODEL),
        'xattn_pre_g': gain(ks[14], D_MODEL),
        'mem_g': gain(ks[15], D_MODEL),
        'xattn_w_q': w(ks[16], (L, D_MODEL, XATTN_WIDTH), D_MODEL),
        'xattn_w_kv': w(ks[17], (L, D_MODEL, 2 * XATTN_WIDTH), D_MODEL),
        'xattn_w_o': w(ks[18], (L, XATTN_WIDTH, D_MODEL), XATTN_WIDTH),
        'xattn_post_g': gain(ks[19], D_MODEL),
        'ffn2_pre_g': gain(ks[20], D_MODEL),
        'ffn2_w_gu': w(ks[21], (L, D_MODEL, 2 * D_FF), D_MODEL),
        'ffn2_w_down': w(ks[22], (L, D_FF, D_MODEL), D_FF),
        'ffn2_post_g': gain(ks[23], D_MODEL),
    }


def reference(x, mem, ffn1_pre_g, ffn1_w_gu, ffn1_w_down, ffn1_post_g, mix_pre_g, w_in,
              conv_w_dw, conv_b_dw, conv_ln_g, conv_ln_b, w_out, mix_post_g,
              xattn_pre_g, mem_g, xattn_w_q, xattn_w_kv, xattn_w_o, xattn_post_g,
              ffn2_pre_g, ffn2_w_gu, ffn2_w_down, ffn2_post_g):
    B_, S_, _ = x.shape
    cos, sin = rope_tables(S_, HEAD_DIM)
    split_at = [CONV_WIDTH, 2 * CONV_WIDTH, 2 * CONV_WIDTH + ATTN_WIDTH, 2 * CONV_WIDTH + 2 * ATTN_WIDTH]
    h = x
    for l in range(DEPTH):
        f = swiglu_ffn(rms_norm(h, ffn1_pre_g[l]), ffn1_w_gu[l], ffn1_w_down[l])
        h = h + FFN_RES_SCALE * rms_norm(f, ffn1_post_g[l])

        u = rms_norm(h, mix_pre_g[l])
        z = u @ w_in[l]
        conv_a, conv_g, q, k, v = jnp.split(z, split_at, axis=-1)
        conv_out = conformer_conv_group(conv_a, conv_g, conv_w_dw[l], conv_b_dw[l],
                                        conv_ln_g[l], conv_ln_b[l])
        heads = lambda t: t.reshape(B_, S_, N_ATTN_HEADS, HEAD_DIM).transpose(0, 2, 1, 3)
        q = apply_rope(heads(q), cos, sin)
        k = apply_rope(heads(k), cos, sin)
        attn = moba_attention(q, k, heads(v))
        attn = attn.transpose(0, 2, 1, 3).reshape(B_, S_, ATTN_WIDTH)
        y = jnp.concatenate([conv_out, attn.astype(conv_out.dtype)], axis=-1) @ w_out[l]
        h = h + rms_norm(y, mix_post_g[l])

        c = memory_cross_attention(rms_norm(h, xattn_pre_g[l]), rms_norm(mem, mem_g[l]),
                                   xattn_w_q[l], xattn_w_kv[l], xattn_w_o[l])
        h = h + rms_norm(c, xattn_post_g[l])

        f = swiglu_ffn(rms_norm(h, ffn2_pre_g[l]), ffn2_w_gu[l], ffn2_w_down[l])
        h = h + FFN_RES_SCALE * rms_norm(f, ffn2_post_g[l])
    return h
```

```python
import functools

import jax
import jax.numpy as jnp
from jax import lax
from jax.experimental import pallas as pl
from jax.experimental.pallas import tpu as pltpu

F32 = jnp.float32
BF16 = jnp.bfloat16

HEAD_DIM = 128
CONV_KERNEL = 31
MOBA_BLOCK = 256
MOBA_TOPK = 3
N_XATTN_HEADS = 4
ROPE_THETA = 10000.0
RMS_EPS = 1e-6
LN_EPS = 1e-5
FFN_RES_SCALE = 0.5
NEG_INF = -1e30

V7X_VMEM_BYTES = 64 * 1024 * 1024
VMEM_LIMIT_BYTES = V7X_VMEM_BYTES - 8 * 1024 * 1024
SUBLANES = 8
CONV_PAD = 32

_NT = (((1,), (1,)), ((), ()))


def _params(*semantics):
    return pltpu.CompilerParams(dimension_semantics=semantics, vmem_limit_bytes=VMEM_LIMIT_BYTES)


def _rms(x, g):
    return x * lax.rsqrt(jnp.mean(x * x, axis=-1, keepdims=True) + RMS_EPS) * g


def _dot(a, b):
    return jnp.dot(a, b, preferred_element_type=F32)


def _ffn_kernel(x_ref, pre_g_ref, wg_ref, wu_ref, wd_ref, post_g_ref, o_ref, xn_ref, acc_ref):
    j = pl.program_id(1)

    @pl.when(j == 0)
    def _():
        xn_ref[...] = _rms(x_ref[...], pre_g_ref[...]).astype(BF16)

    xn = xn_ref[...]
    gate = _dot(xn, wg_ref[...])
    up = _dot(xn, wu_ref[...])
    act = (gate * jax.nn.sigmoid(gate) * up).astype(BF16)
    part = _dot(act, wd_ref[...])

    @pl.when(j == 0)
    def _():
        acc_ref[...] = part

    @pl.when(j > 0)
    def _():
        acc_ref[...] += part

    @pl.when(j == pl.num_programs(1) - 1)
    def _():
        o_ref[...] = x_ref[...] + FFN_RES_SCALE * _rms(acc_ref[...], post_g_ref[...])


def _ffn(h, pre_g, w_gu, w_down, post_g, *, tm, tf):
    m, d = h.shape
    d_ff = w_down.shape[0]
    nf = d_ff // tf
    return pl.pallas_call(
        _ffn_kernel,
        grid=(m // tm, nf),
        in_specs=[
            pl.BlockSpec((tm, d), lambda i, j: (i, 0)),
            pl.BlockSpec((1, d), lambda i, j: (0, 0)),
            pl.BlockSpec((d, tf), lambda i, j: (0, j)),
            pl.BlockSpec((d, tf), lambda i, j: (0, j + nf)),
            pl.BlockSpec((tf, d), lambda i, j: (j, 0)),
            pl.BlockSpec((1, d), lambda i, j: (0, 0)),
        ],
        out_specs=pl.BlockSpec((tm, d), lambda i, j: (i, 0)),
        out_shape=jax.ShapeDtypeStruct((m, d), F32),
        scratch_shapes=[pltpu.VMEM((tm, d), BF16), pltpu.VMEM((tm, d), F32)],
        compiler_params=_params("parallel", "arbitrary"),
        name="ffn",
    )(h, pre_g, w_gu, w_gu, w_down, post_g)


def _in_proj_kernel(x_ref, g_ref, w_ref, wgate_ref, cos_ref, sin_ref,
                    y_ref, q_ref, k_ref, v_ref, xn_ref):
    j = pl.program_id(1)

    @pl.when(j == 0)
    def _():
        xn_ref[...] = _rms(x_ref[...], g_ref[...]).astype(BF16)

    xn = xn_ref[...]
    z = _dot(xn, w_ref[...])
    n_heads = z.shape[1] // HEAD_DIM

    def rope(o_ref):
        cos, sin = cos_ref[...], sin_ref[...]
        for h in range(n_heads):
            cols = slice(h * HEAD_DIM, (h + 1) * HEAD_DIM)
            zh = z[:, cols]
            o_ref[:, cols] = zh * cos + pltpu.roll(zh, HEAD_DIM // 2, axis=1) * sin

    @pl.when(j == 0)
    def _():
        y_ref[...] = z * jax.nn.sigmoid(_dot(xn, wgate_ref[...]))

    @pl.when(j == 1)
    def _():
        rope(q_ref)

    @pl.when(j == 2)
    def _():
        rope(k_ref)

    @pl.when(j == 3)
    def _():
        v_ref[...] = z.astype(BF16)


def _in_proj(h, g, w_in, cos, sin, *, tm, seq):
    m, d = h.shape
    width = w_in.shape[1] // 5
    row_out = lambda i, j: (i, 0)
    return pl.pallas_call(
        _in_proj_kernel,
        grid=(m // tm, 4),
        in_specs=[
            pl.BlockSpec((tm, d), row_out),
            pl.BlockSpec((1, d), lambda i, j: (0, 0)),
            pl.BlockSpec((d, width), lambda i, j: (0, jnp.where(j == 0, 0, j + 1))),
            pl.BlockSpec((d, width), lambda i, j: (0, 1)),
            pl.BlockSpec((tm, HEAD_DIM), lambda i, j: (i % (seq // tm), 0)),
            pl.BlockSpec((tm, HEAD_DIM), lambda i, j: (i % (seq // tm), 0)),
        ],
        out_specs=[pl.BlockSpec((tm, width), row_out)] * 4,
        out_shape=[
            jax.ShapeDtypeStruct((m, width), F32),
            jax.ShapeDtypeStruct((m, width), F32),
            jax.ShapeDtypeStruct((m, width), F32),
            jax.ShapeDtypeStruct((m, width), BF16),
        ],
        scratch_shapes=[pltpu.VMEM((tm, d), BF16)],
        compiler_params=_params("parallel", "arbitrary"),
        name="in_proj",
    )(h, g, w_in, w_in, cos, sin)


def _conv_kernel(y_ref, w_ref, b_ref, ln_g_ref, ln_b_ref, o_ref, win_ref, *, rows):
    t = pl.program_id(1)
    ts = y_ref.shape[0]

    @pl.when(t == 0)
    def _():
        win_ref[0:CONV_PAD, :] = jnp.zeros((CONV_PAD, win_ref.shape[1]), F32)

    @pl.when(t > 0)
    def _():
        win_ref[0:CONV_PAD, :] = win_ref[ts:ts + CONV_PAD, :]

    win_ref[CONV_PAD:CONV_PAD + ts, :] = y_ref[...]

    first = CONV_PAD - (CONV_KERNEL - 1)
    for c in range(ts // rows):
        r0 = c * rows
        acc = jnp.broadcast_to(b_ref[...], (rows, win_ref.shape[1]))
        for tap in range(CONV_KERNEL):
            acc = acc + win_ref[r0 + first + tap:r0 + first + tap + rows, :] * w_ref[tap:tap + 1, :]
        mu = jnp.mean(acc, axis=-1, keepdims=True)
        xc = acc - mu
        var = jnp.mean(xc * xc, axis=-1, keepdims=True)
        yn = xc * lax.rsqrt(var + LN_EPS) * ln_g_ref[...] + ln_b_ref[...]
        o_ref[r0:r0 + rows, :] = (yn * jax.nn.sigmoid(yn)).astype(o_ref.dtype)


def _conv_group(y, w_dw, b_dw, ln_g, ln_b, *, ts, seq, rows):
    m, c = y.shape
    nt = seq // ts
    const = lambda b, t: (0, 0)
    return pl.pallas_call(
        functools.partial(_conv_kernel, rows=rows),
        grid=(m // seq, nt),
        in_specs=[
            pl.BlockSpec((ts, c), lambda b, t: (b * nt + t, 0)),
            pl.BlockSpec((CONV_KERNEL, c), const),
            pl.BlockSpec((1, c), const),
            pl.BlockSpec((1, c), const),
            pl.BlockSpec((1, c), const),
        ],
        out_specs=pl.BlockSpec((ts, c), lambda b, t: (b * nt + t, 0)),
        out_shape=jax.ShapeDtypeStruct((m, c), BF16),
        scratch_shapes=[pltpu.VMEM((CONV_PAD + ts, c), F32)],
        compiler_params=_params("parallel", "arbitrary"),
        name="conv_group",
    )(y, w_dw, b_dw, ln_g, ln_b)


def _moba_kernel(q_ref, k_ref, v_ref, o_ref, kb_ref, vt_ref, kmean_ref):
    bs = MOBA_BLOCK
    nb = q_ref.shape[0] // bs
    scale = HEAD_DIM ** -0.5

    for j in range(nb):
        kj = k_ref[j * bs:(j + 1) * bs, :]
        kb_ref[j] = kj.astype(BF16)
        kmean_ref[j:j + 1, :] = jnp.mean(kj, axis=0, keepdims=True)
        vt_ref[j] = v_ref[j * bs:(j + 1) * bs, :].astype(F32).T.astype(BF16)

    kmean = kmean_ref[...]
    blk = lax.broadcasted_iota(jnp.int32, (nb, bs), 0)
    key_pos = lax.broadcasted_iota(jnp.int32, (bs, bs), 0)
    qry_pos = lax.broadcasted_iota(jnp.int32, (bs, bs), 1)

    for qi in range(nb):
        q = q_ref[qi * bs:(qi + 1) * bs, :]
        qb = q.astype(BF16)

        s = lax.dot_general(kb_ref[qi], qb, _NT, preferred_element_type=F32) * scale
        s = jnp.where(key_pos <= qry_pos, s, NEG_INF)
        m = jnp.max(s, axis=0, keepdims=True)
        p = jnp.exp(s - m)
        l = jnp.sum(p, axis=0, keepdims=True)
        acc = _dot(vt_ref[qi], p.astype(BF16))

        if qi > 0:
            gate = lax.dot_general(kmean, q, _NT, precision=lax.Precision.HIGHEST,
                                   preferred_element_type=F32)
            beaten_by = jnp.zeros((nb, bs), jnp.int32)
            for i in range(qi):
                gi = gate[i:i + 1, :]
                beats = (gi > gate) | ((gi == gate) & (i < blk))
                beaten_by = beaten_by + beats.astype(jnp.int32)
            keep = jnp.where((blk < qi) & (beaten_by < MOBA_TOPK), 1.0, 0.0)

            for j in range(qi):
                s = lax.dot_general(kb_ref[j], qb, _NT, preferred_element_type=F32) * scale
                s = jnp.where(keep[j:j + 1, :] > 0.5, s, NEG_INF)
                m_new = jnp.maximum(m, jnp.max(s, axis=0, keepdims=True))
                alpha = jnp.exp(m - m_new)
                p = jnp.exp(s - m_new)
                l = alpha * l + jnp.sum(p, axis=0, keepdims=True)
                acc = alpha * acc + _dot(vt_ref[j], p.astype(BF16))
                m = m_new

        o_ref[qi * bs:(qi + 1) * bs, :] = (acc / l).T.astype(o_ref.dtype)


def _moba(q, k, v, *, seq):
    m, width = q.shape
    n_heads = width // HEAD_DIM
    nb = seq // MOBA_BLOCK
    spec = pl.BlockSpec((seq, HEAD_DIM), lambda b, h: (b, h))
    return pl.pallas_call(
        _moba_kernel,
        grid=(m // seq, n_heads),
        in_specs=[spec, spec, spec],
        out_specs=spec,
        out_shape=jax.ShapeDtypeStruct((m, width), BF16),
        scratch_shapes=[
            pltpu.VMEM((nb, MOBA_BLOCK, HEAD_DIM), BF16),
            pltpu.VMEM((nb, HEAD_DIM, MOBA_BLOCK), BF16),
            pltpu.VMEM((nb, HEAD_DIM), F32),
        ],
        compiler_params=_params("parallel", "parallel"),
        name="moba",
    )(q, k, v)


def _norm_matmul_kernel(x_ref, g_ref, w_ref, o_ref):
    xn = _rms(x_ref[...], g_ref[...]).astype(BF16)
    o_ref[...] = _dot(xn, w_ref[...]).astype(o_ref.dtype)


def _norm_matmul(x, g, w, *, tm, out_dtype):
    m, d = x.shape
    n = w.shape[1]
    return pl.pallas_call(
        _norm_matmul_kernel,
        grid=(m // tm,),
        in_specs=[
            pl.BlockSpec((tm, d), lambda i: (i, 0)),
            pl.BlockSpec((1, d), lambda i: (0, 0)),
            pl.BlockSpec((d, n), lambda i: (0, 0)),
        ],
        out_specs=pl.BlockSpec((tm, n), lambda i: (i, 0)),
        out_shape=jax.ShapeDtypeStruct((m, n), out_dtype),
        compiler_params=_params("parallel"),
        name="mem_kv",
    )(x, g, w)


def _mix_out_kernel(conv_ref, attn_ref, h_ref, w_conv_ref, w_attn_ref, mix_g_ref,
                    xpre_g_ref, wq_ref, kv_ref, wo_ref, xpost_g_ref, o_ref):
    y = _dot(conv_ref[...], w_conv_ref[...]) + _dot(attn_ref[...], w_attn_ref[...])
    h = h_ref[...] + _rms(y, mix_g_ref[...])

    u = _rms(h, xpre_g_ref[...]).astype(BF16)
    q = _dot(u, wq_ref[...])
    width = q.shape[1]
    scale = HEAD_DIM ** -0.5
    heads = []
    for hd in range(N_XATTN_HEADS):
        cols = slice(hd * HEAD_DIM, (hd + 1) * HEAD_DIM)
        vcols = slice(width + hd * HEAD_DIM, width + (hd + 1) * HEAD_DIM)
        s = lax.dot_general(q[:, cols].astype(BF16), kv_ref[:, cols], _NT,
                            preferred_element_type=F32) * scale
        e = jnp.exp(s - jnp.max(s, axis=-1, keepdims=True))
        p = e / jnp.sum(e, axis=-1, keepdims=True)
        heads.append(_dot(p.astype(BF16), kv_ref[:, vcols]))
    o = jnp.concatenate(heads, axis=1).astype(BF16)
    c = _dot(o, wo_ref[...])
    o_ref[...] = h + _rms(c, xpost_g_ref[...])


def _mix_out(conv, attn, h, w_out, mix_g, xpre_g, w_q, kv, w_o, xpost_g, *, tm, seq):
    m, d = h.shape
    cw = conv.shape[1]
    aw = attn.shape[1]
    mem_len = kv.shape[0] // (m // seq)
    const = lambda i: (0, 0)
    row = lambda i: (i, 0)
    return pl.pallas_call(
        _mix_out_kernel,
        grid=(m // tm,),
        in_specs=[
            pl.BlockSpec((tm, cw), row),
            pl.BlockSpec((tm, aw), row),
            pl.BlockSpec((tm, d), row),
            pl.BlockSpec((cw, d), const),
            pl.BlockSpec((aw, d), lambda i: (cw // aw, 0)),
            pl.BlockSpec((1, d), const),
            pl.BlockSpec((1, d), const),
            pl.BlockSpec(w_q.shape, const),
            pl.BlockSpec((mem_len, kv.shape[1]), lambda i: (i // (seq // tm), 0)),
            pl.BlockSpec(w_o.shape, const),
            pl.BlockSpec((1, d), const),
        ],
        out_specs=pl.BlockSpec((tm, d), row),
        out_shape=jax.ShapeDtypeStruct((m, d), F32),
        compiler_params=_params("parallel"),
        name="mix_out",
    )(conv, attn, h, w_out, w_out, mix_g, xpre_g, w_q, kv, w_o, xpost_g)


def _rope_tables(seq):
    inv_freq = 1.0 / (ROPE_THETA ** (jnp.arange(0, HEAD_DIM, 2, dtype=F32) / HEAD_DIM))
    ang = jnp.arange(seq, dtype=F32)[:, None] * inv_freq[None, :]
    cos, sin = jnp.cos(ang), jnp.sin(ang)
    return jnp.concatenate([cos, cos], axis=-1), jnp.concatenate([-sin, sin], axis=-1)


def kernel(x, mem, ffn1_pre_g, ffn1_w_gu, ffn1_w_down, ffn1_post_g, mix_pre_g, w_in, conv_w_dw, conv_b_dw, conv_ln_g, conv_ln_b, w_out, mix_post_g, xattn_pre_g, mem_g, xattn_w_q, xattn_w_kv, xattn_w_o, xattn_post_g, ffn2_pre_g, ffn2_w_gu, ffn2_w_down, ffn2_post_g):
    batch, seq, d = x.shape
    mem_len = mem.shape[1]
    cos, sin = _rope_tables(seq)
    row = lambda a, l: a[l][None, :]
    h = x.reshape(batch * seq, d)
    mem2d = mem.reshape(batch * mem_len, d)

    for l in range(ffn1_w_gu.shape[0]):
        h = _ffn(h, row(ffn1_pre_g, l), ffn1_w_gu[l].astype(BF16), ffn1_w_down[l].astype(BF16),
                 row(ffn1_post_g, l), tm=512, tf=512)

        y, q, k, v = _in_proj(h, row(mix_pre_g, l), w_in[l].astype(BF16), cos, sin, tm=512, seq=seq)
        conv = _conv_group(y, conv_w_dw[l], row(conv_b_dw, l), row(conv_ln_g, l), row(conv_ln_b, l),
                           ts=256, seq=seq, rows=16)
        attn = _moba(q, k, v, seq=seq)

        kv = _norm_matmul(mem2d, row(mem_g, l), xattn_w_kv[l].astype(BF16), tm=256, out_dtype=BF16)
        h = _mix_out(conv, attn, h, w_out[l].astype(BF16), row(mix_post_g, l), row(xattn_pre_g, l),
                     xattn_w_q[l].astype(BF16), kv, xattn_w_o[l].astype(BF16), row(xattn_post_g, l),
                     tm=256, seq=seq)

        h = _ffn(h, row(ffn2_pre_g, l), ffn2_w_gu[l].astype(BF16), ffn2_w_down[l].astype(BF16),
                 row(ffn2_post_g, l), tm=512, tf=512)
    return h.reshape(batch, seq, d)
```

```python
import functools
import math

import jax
import jax.numpy as jnp
from jax import lax
from jax.experimental import pallas as pl
from jax.experimental.pallas import tpu as pltpu

F32 = jnp.float32
BF16 = jnp.bfloat16

HEAD_DIM = 128
CONV_KERNEL = 31
MOBA_BLOCK = 256
MOBA_TOPK = 3
N_XATTN_HEADS = 4
ROPE_THETA = 10000.0
RMS_EPS = 1e-6
LN_EPS = 1e-5
FFN_RES_SCALE = 0.5
NEG_INF = -1e30

V7X_VMEM_BYTES = 64 * 1024 * 1024
VMEM_LIMIT_BYTES = V7X_VMEM_BYTES - 8 * 1024 * 1024
SUBLANES = 8
CONV_PAD = 32

_NT = (((1,), (1,)), ((), ()))


def _params(*semantics):
    return pltpu.CompilerParams(dimension_semantics=semantics, vmem_limit_bytes=VMEM_LIMIT_BYTES)


def _resident(shape):
    return pl.BlockSpec(shape, lambda *_: (0,) * len(shape), pipeline_mode=pl.Buffered(1))


def _rms(x, g):
    return x * lax.rsqrt(jnp.mean(x * x, axis=-1, keepdims=True) + RMS_EPS) * g


def _dot(a, b):
    return jnp.dot(a, b, preferred_element_type=F32)


def _ffn_kernel(x_ref, pre_g_ref, wg_ref, wu_ref, wd_ref, post_g_ref, o_ref, xn_ref):
    j = pl.program_id(1)

    @pl.when(j == 0)
    def _():
        xn_ref[...] = _rms(x_ref[...], pre_g_ref[...]).astype(BF16)
        o_ref[...] = jnp.zeros(o_ref.shape, F32)

    xn = xn_ref[...]
    gate = _dot(xn, wg_ref[...])
    up = _dot(xn, wu_ref[...])
    act = (gate * jax.nn.sigmoid(gate) * up).astype(BF16)
    o_ref[...] += _dot(act, wd_ref[...])

    @pl.when(j == pl.num_programs(1) - 1)
    def _():
        o_ref[...] = x_ref[...] + FFN_RES_SCALE * _rms(o_ref[...], post_g_ref[...])


def _ffn(h, pre_g, w_gu, w_down, post_g, *, tm, tf):
    m, d = h.shape
    d_ff = w_down.shape[0]
    nf = d_ff // tf
    return pl.pallas_call(
        _ffn_kernel,
        grid=(m // tm, nf),
        in_specs=[
            pl.BlockSpec((tm, d), lambda i, j: (i, 0)),
            pl.BlockSpec((1, d), lambda i, j: (0, 0)),
            pl.BlockSpec((d, tf), lambda i, j: (0, j)),
            pl.BlockSpec((d, tf), lambda i, j: (0, j + nf)),
            pl.BlockSpec((tf, d), lambda i, j: (j, 0)),
            pl.BlockSpec((1, d), lambda i, j: (0, 0)),
        ],
        out_specs=pl.BlockSpec((tm, d), lambda i, j: (i, 0)),
        out_shape=jax.ShapeDtypeStruct((m, d), F32),
        scratch_shapes=[pltpu.VMEM((tm, d), BF16)],
        compiler_params=_params("parallel", "arbitrary"),
        name="ffn",
    )(h, pre_g, w_gu, w_gu, w_down, post_g)


def _in_proj_kernel(x_ref, g_ref, w_ref, cos_ref, sin_ref, y_ref, q_ref, k_ref, v_ref, *, sub):
    width = y_ref.shape[1]
    n_heads = width // HEAD_DIM
    col = lambda n: w_ref[:, n * width:(n + 1) * width]

    for r in range(x_ref.shape[0] // sub):
        rows = slice(r * sub, (r + 1) * sub)
        xn = _rms(x_ref[rows, :], g_ref[...]).astype(BF16)
        cos, sin = cos_ref[rows, :], sin_ref[rows, :]

        def rope(z, o_ref):
            for h in range(n_heads):
                cols = slice(h * HEAD_DIM, (h + 1) * HEAD_DIM)
                zh = z[:, cols]
                o_ref[rows, cols] = zh * cos + pltpu.roll(zh, HEAD_DIM // 2, axis=1) * sin

        y_ref[rows, :] = _dot(xn, col(0)) * jax.nn.sigmoid(_dot(xn, col(1)))
        rope(_dot(xn, col(2)), q_ref)
        rope(_dot(xn, col(3)), k_ref)
        v_ref[rows, :] = _dot(xn, col(4)).astype(BF16)


def _in_proj(h, g, w_in, cos, sin, *, tm, seq, sub):
    m, d = h.shape
    width = w_in.shape[1] // 5
    row = lambda i: (i, 0)
    table = pl.BlockSpec((tm, HEAD_DIM), lambda i: (i % (seq // tm), 0))
    return pl.pallas_call(
        functools.partial(_in_proj_kernel, sub=sub),
        grid=(m // tm,),
        in_specs=[pl.BlockSpec((tm, d), row), _resident((1, d)), _resident(w_in.shape), table, table],
        out_specs=[pl.BlockSpec((tm, width), row)] * 4,
        out_shape=[
            jax.ShapeDtypeStruct((m, width), F32),
            jax.ShapeDtypeStruct((m, width), F32),
            jax.ShapeDtypeStruct((m, width), F32),
            jax.ShapeDtypeStruct((m, width), BF16),
        ],
        compiler_params=_params("parallel"),
        name="in_proj",
    )(h, g, w_in, cos, sin)


def _conv_kernel(y_ref, w_ref, b_ref, ln_g_ref, ln_b_ref, o_ref, win_ref, shift_ref, w8_ref, conv_ref,
                 *, rows):
    t = pl.program_id(1)
    ts, c = y_ref.shape

    @pl.when(t == 0)
    def _():
        win_ref[0:CONV_PAD, :] = jnp.zeros((CONV_PAD, c), F32)
        for tap in range(CONV_KERNEL):
            w8_ref[tap] = jnp.broadcast_to(w_ref[tap:tap + 1, :], (SUBLANES, c))

    @pl.when(t > 0)
    def _():
        win_ref[0:CONV_PAD, :] = win_ref[ts:ts + CONV_PAD, :]

    win_ref[CONV_PAD:CONV_PAD + ts, :] = y_ref[...]
    span = shift_ref.shape[1]
    for b in range(1, SUBLANES):
        shift_ref[b - 1] = win_ref[b:b + span, :]

    first = CONV_PAD - (CONV_KERNEL - 1)
    n_sub = rows // SUBLANES

    def chunk(ci, carry):
        r0 = pl.multiple_of(ci * rows, rows)
        accs = [jnp.broadcast_to(b_ref[...], (SUBLANES, c))] * n_sub
        for tap in range(CONV_KERNEL):
            off, b = divmod(first + tap, SUBLANES)
            src = win_ref if b == 0 else shift_ref.at[b - 1]
            wt = w8_ref[tap]
            for i in range(n_sub):
                start = pl.multiple_of(r0 + (off + i) * SUBLANES, SUBLANES)
                accs[i] = accs[i] + src[pl.ds(start, SUBLANES), :] * wt
        for i in range(n_sub):
            conv_ref[pl.ds(pl.multiple_of(r0 + i * SUBLANES, SUBLANES), SUBLANES), :] = accs[i]
        return carry

    lax.fori_loop(0, ts // rows, chunk, 0)

    acc = conv_ref[...]
    mu = jnp.mean(acc, axis=-1, keepdims=True)
    xc = acc - mu
    var = jnp.mean(xc * xc, axis=-1, keepdims=True)
    yn = xc * lax.rsqrt(var + LN_EPS) * ln_g_ref[...] + ln_b_ref[...]
    o_ref[...] = (yn * jax.nn.sigmoid(yn)).astype(o_ref.dtype)


def _conv_group(y, w_dw, b_dw, ln_g, ln_b, *, ts, seq, rows):
    m, c = y.shape
    nt = seq // ts
    const = lambda b, t: (0, 0)
    return pl.pallas_call(
        functools.partial(_conv_kernel, rows=rows),
        grid=(m // seq, nt),
        in_specs=[
            pl.BlockSpec((ts, c), lambda b, t: (b * nt + t, 0)),
            pl.BlockSpec((CONV_KERNEL, c), const),
            pl.BlockSpec((1, c), const),
            pl.BlockSpec((1, c), const),
            pl.BlockSpec((1, c), const),
        ],
        out_specs=pl.BlockSpec((ts, c), lambda b, t: (b * nt + t, 0)),
        out_shape=jax.ShapeDtypeStruct((m, c), BF16),
        scratch_shapes=[
            pltpu.VMEM((CONV_PAD + ts, c), F32),
            pltpu.VMEM((SUBLANES - 1, CONV_PAD + ts - SUBLANES, c), F32),
            pltpu.VMEM((CONV_KERNEL, SUBLANES, c), F32),
            pltpu.VMEM((ts, c), F32),
        ],
        compiler_params=_params("parallel", "arbitrary"),
        name="conv_group",
    )(y, w_dw, b_dw, ln_g, ln_b)


def _moba_kernel(q_ref, k_ref, v_ref, o_ref, kb_ref, vt_ref, kmean_ref, p_ref):
    bs = MOBA_BLOCK
    nb = q_ref.shape[0] // bs
    c = HEAD_DIM ** -0.5 * math.log2(math.e)

    for j in range(nb):
        blk_rows = slice(j * bs, (j + 1) * bs)
        kj = k_ref[blk_rows, :]
        kb_ref[blk_rows, :] = kj.astype(BF16)
        kmean_ref[j:j + 1, :] = jnp.mean(kj, axis=0, keepdims=True)
        vt_ref[:, blk_rows] = v_ref[blk_rows, :].astype(F32).T.astype(BF16)

    kmean = kmean_ref[...]
    blk = lax.broadcasted_iota(jnp.int32, (nb, bs), 0)
    causal = (lax.broadcasted_iota(jnp.int32, (bs, bs), 0)
              <= lax.broadcasted_iota(jnp.int32, (bs, bs), 1))

    for qi in range(nb):
        nk = (qi + 1) * bs
        q = q_ref[qi * bs:(qi + 1) * bs, :]
        s = lax.dot_general(kb_ref[0:nk, :], q.astype(BF16), _NT, preferred_element_type=F32) * c

        if qi > 0:
            gate = lax.dot_general(kmean, q, _NT, precision=lax.Precision.HIGHEST,
                                   preferred_element_type=F32)
            beaten_by = jnp.zeros((nb, bs), jnp.int32)
            for i in range(qi):
                gi = gate[i:i + 1, :]
                beats = (gi > gate) | ((gi == gate) & (i < blk))
                beaten_by = beaten_by + beats.astype(jnp.int32)
            keep = jnp.where((blk < qi) & (beaten_by < MOBA_TOPK), 1.0, 0.0)

        blocks = []
        for j in range(qi + 1):
            sj = s[j * bs:(j + 1) * bs, :]
            if j == qi:
                sj = jnp.where(causal, sj, NEG_INF)
            else:
                sj = jnp.where(keep[j:j + 1, :] > 0.5, sj, NEG_INF)
            blocks.append(sj)

        m = functools.reduce(jnp.maximum, [jnp.max(sj, axis=0, keepdims=True) for sj in blocks])
        l = jnp.zeros((1, bs), F32)
        for j, sj in enumerate(blocks):
            p = jnp.exp2(sj - m)
            l = l + jnp.sum(p, axis=0, keepdims=True)
            p_ref[qi % 2, j * bs:(j + 1) * bs, :] = p.astype(BF16)

        acc = _dot(vt_ref[:, 0:nk], p_ref[qi % 2, 0:nk, :])
        o_ref[qi * bs:(qi + 1) * bs, :] = (acc / l).T.astype(o_ref.dtype)


def _moba(q, k, v, *, seq):
    m, width = q.shape
    n_heads = width // HEAD_DIM
    spec = pl.BlockSpec((seq, HEAD_DIM), lambda b, h: (b, h))
    return pl.pallas_call(
        _moba_kernel,
        grid=(m // seq, n_heads),
        in_specs=[spec, spec, spec],
        out_specs=spec,
        out_shape=jax.ShapeDtypeStruct((m, width), BF16),
        scratch_shapes=[
            pltpu.VMEM((seq, HEAD_DIM), BF16),
            pltpu.VMEM((HEAD_DIM, seq), BF16),
            pltpu.VMEM((seq // MOBA_BLOCK, HEAD_DIM), F32),
            pltpu.VMEM((2, seq, MOBA_BLOCK), BF16),
        ],
        compiler_params=_params("parallel", "parallel"),
        name="moba",
    )(q, k, v)


def _norm_matmul_kernel(x_ref, g_ref, w_ref, o_ref):
    xn = _rms(x_ref[...], g_ref[...]).astype(BF16)
    o_ref[...] = _dot(xn, w_ref[...]).astype(o_ref.dtype)


def _norm_matmul(x, g, w, *, tm, out_dtype):
    m, d = x.shape
    n = w.shape[1]
    return pl.pallas_call(
        _norm_matmul_kernel,
        grid=(m // tm,),
        in_specs=[pl.BlockSpec((tm, d), lambda i: (i, 0)), _resident((1, d)), _resident((d, n))],
        out_specs=pl.BlockSpec((tm, n), lambda i: (i, 0)),
        out_shape=jax.ShapeDtypeStruct((m, n), out_dtype),
        compiler_params=_params("parallel"),
        name="mem_kv",
    )(x, g, w)


def _mix_out_kernel(conv_ref, attn_ref, h_ref, w_out_ref, mix_g_ref,
                    xpre_g_ref, wq_ref, kv_ref, wo_ref, xpost_g_ref, o_ref):
    cw = conv_ref.shape[1]
    y = _dot(conv_ref[...], w_out_ref[0:cw, :]) + _dot(attn_ref[...], w_out_ref[cw:, :])
    h = h_ref[...] + _rms(y, mix_g_ref[...])

    u = _rms(h, xpre_g_ref[...]).astype(BF16)
    q = _dot(u, wq_ref[...])
    width = q.shape[1]
    scale = HEAD_DIM ** -0.5
    heads = []
    for hd in range(N_XATTN_HEADS):
        cols = slice(hd * HEAD_DIM, (hd + 1) * HEAD_DIM)
        vcols = slice(width + hd * HEAD_DIM, width + (hd + 1) * HEAD_DIM)
        s = lax.dot_general(q[:, cols].astype(BF16), kv_ref[:, cols], _NT,
                            preferred_element_type=F32) * scale
        e = jnp.exp(s - jnp.max(s, axis=-1, keepdims=True))
        p = e / jnp.sum(e, axis=-1, keepdims=True)
        heads.append(_dot(p.astype(BF16), kv_ref[:, vcols]))
    o = jnp.concatenate(heads, axis=1).astype(BF16)
    c = _dot(o, wo_ref[...])
    o_ref[...] = h + _rms(c, xpost_g_ref[...])


def _mix_out(conv, attn, h, w_out, mix_g, xpre_g, w_q, kv, w_o, xpost_g, *, tm, seq):
    m, d = h.shape
    mem_len = kv.shape[0] // (m // seq)
    row = lambda i: (i, 0)
    return pl.pallas_call(
        _mix_out_kernel,
        grid=(m // tm,),
        in_specs=[
            pl.BlockSpec((tm, conv.shape[1]), row),
            pl.BlockSpec((tm, attn.shape[1]), row),
            pl.BlockSpec((tm, d), row),
            _resident(w_out.shape),
            _resident((1, d)),
            _resident((1, d)),
            _resident(w_q.shape),
            pl.BlockSpec((mem_len, kv.shape[1]), lambda i: (i // (seq // tm), 0)),
            _resident(w_o.shape),
            _resident((1, d)),
        ],
        out_specs=pl.BlockSpec((tm, d), row),
        out_shape=jax.ShapeDtypeStruct((m, d), F32),
        compiler_params=_params("parallel"),
        name="mix_out",
    )(conv, attn, h, w_out, mix_g, xpre_g, w_q, kv, w_o, xpost_g)


def _rope_tables(seq):
    inv_freq = 1.0 / (ROPE_THETA ** (jnp.arange(0, HEAD_DIM, 2, dtype=F32) / HEAD_DIM))
    ang = jnp.arange(seq, dtype=F32)[:, None] * inv_freq[None, :]
    cos, sin = jnp.cos(ang), jnp.sin(ang)
    return jnp.concatenate([cos, cos], axis=-1), jnp.concatenate([-sin, sin], axis=-1)


def kernel(x, mem, ffn1_pre_g, ffn1_w_gu, ffn1_w_down, ffn1_post_g, mix_pre_g, w_in, conv_w_dw, conv_b_dw, conv_ln_g, conv_ln_b, w_out, mix_post_g, xattn_pre_g, mem_g, xattn_w_q, xattn_w_kv, xattn_w_o, xattn_post_g, ffn2_pre_g, ffn2_w_gu, ffn2_w_down, ffn2_post_g):
    batch, seq, d = x.shape
    mem_len = mem.shape[1]
    cos, sin = _rope_tables(seq)
    row = lambda a, l: a[l][None, :]
    h = x.reshape(batch * seq, d)
    mem2d = mem.reshape(batch * mem_len, d)

    for l in range(ffn1_w_gu.shape[0]):
        h = _ffn(h, row(ffn1_pre_g, l), ffn1_w_gu[l].astype(BF16), ffn1_w_down[l].astype(BF16),
                 row(ffn1_post_g, l), tm=512, tf=512)

        y, q, k, v = _in_proj(h, row(mix_pre_g, l), w_in[l].astype(BF16), cos, sin,
                              tm=512, seq=seq, sub=256)
        conv = _conv_group(y, conv_w_dw[l], row(conv_b_dw, l), row(conv_ln_g, l), row(conv_ln_b, l),
                           ts=256, seq=seq, rows=32)
        attn = _moba(q, k, v, seq=seq)

        kv = _norm_matmul(mem2d, row(mem_g, l), xattn_w_kv[l].astype(BF16), tm=256, out_dtype=BF16)
        h = _mix_out(conv, attn, h, w_out[l].astype(BF16), row(mix_post_g, l), row(xattn_pre_g, l),
                     xattn_w_q[l].astype(BF16), kv, xattn_w_o[l].astype(BF16), row(xattn_post_g, l),
                     tm=256, seq=seq)

        h = _ffn(h, row(ffn2_pre_g, l), ffn2_w_gu[l].astype(BF16), ffn2_w_down[l].astype(BF16),
                 row(ffn2_post_g, l), tm=512, tf=512)
    return h.reshape(batch, seq, d)
```

```python
import functools
import math

import jax
import jax.numpy as jnp
from jax import lax
from jax.experimental import pallas as pl
from jax.experimental.pallas import tpu as pltpu

F32 = jnp.float32
BF16 = jnp.bfloat16

HEAD_DIM = 128
CONV_KERNEL = 31
MOBA_BLOCK = 256
MOBA_TOPK = 3
N_XATTN_HEADS = 4
ROPE_THETA = 10000.0
RMS_EPS = 1e-6
LN_EPS = 1e-5
FFN_RES_SCALE = 0.5
NEG_INF = -1e30

V7X_VMEM_BYTES = 64 * 1024 * 1024
VMEM_LIMIT_BYTES = V7X_VMEM_BYTES - 3 * 1024 * 1024
SUBLANES = 8
CONV_PAD = 32

_NT = (((1,), (1,)), ((), ()))


def _params(*semantics):
    return pltpu.CompilerParams(dimension_semantics=semantics, vmem_limit_bytes=VMEM_LIMIT_BYTES)


def _resident(shape):
    return pl.BlockSpec(shape, lambda *_: (0,) * len(shape), pipeline_mode=pl.Buffered(1))


def _rms(x, g):
    return x * lax.rsqrt(jnp.mean(x * x, axis=-1, keepdims=True) + RMS_EPS) * g


def _dot(a, b):
    return jnp.dot(a, b, preferred_element_type=F32)


def _ffn_kernel(x_ref, pre_g_ref, wg_ref, wu_ref, wd_ref, post_g_ref, o_ref, xn_ref):
    j = pl.program_id(1)

    @pl.when(j == 0)
    def _():
        xn_ref[...] = _rms(x_ref[...], pre_g_ref[...]).astype(BF16)
        o_ref[...] = jnp.zeros(o_ref.shape, F32)

    xn = xn_ref[...]
    gate = _dot(xn, wg_ref[...])
    up = _dot(xn, wu_ref[...])
    act = (gate * jax.nn.sigmoid(gate) * up).astype(BF16)
    o_ref[...] += _dot(act, wd_ref[...])

    @pl.when(j == pl.num_programs(1) - 1)
    def _():
        o_ref[...] = x_ref[...] + FFN_RES_SCALE * _rms(o_ref[...], post_g_ref[...])


def _ffn(h, pre_g, w_gu, w_down, post_g, *, tm, tf):
    m, d = h.shape
    d_ff = w_down.shape[0]
    nf = d_ff // tf
    return pl.pallas_call(
        _ffn_kernel,
        grid=(m // tm, nf),
        in_specs=[
            pl.BlockSpec((tm, d), lambda i, j: (i, 0)),
            pl.BlockSpec((1, d), lambda i, j: (0, 0)),
            pl.BlockSpec((d, tf), lambda i, j: (0, j)),
            pl.BlockSpec((d, tf), lambda i, j: (0, j + nf)),
            pl.BlockSpec((tf, d), lambda i, j: (j, 0)),
            pl.BlockSpec((1, d), lambda i, j: (0, 0)),
        ],
        out_specs=pl.BlockSpec((tm, d), lambda i, j: (i, 0)),
        out_shape=jax.ShapeDtypeStruct((m, d), F32),
        scratch_shapes=[pltpu.VMEM((tm, d), BF16)],
        compiler_params=_params("parallel", "arbitrary"),
        name="ffn",
    )(h, pre_g, w_gu, w_gu, w_down, post_g)


def _in_proj_kernel(x_ref, g_ref, w_ref, cos_ref, sin_ref, y_ref, q_ref, k_ref, v_ref, *, sub):
    width = y_ref.shape[1]
    n_heads = width // HEAD_DIM
    col = lambda n: w_ref[:, n * width:(n + 1) * width]

    for r in range(x_ref.shape[0] // sub):
        rows = slice(r * sub, (r + 1) * sub)
        xn = _rms(x_ref[rows, :], g_ref[...]).astype(BF16)
        cos, sin = cos_ref[rows, :], sin_ref[rows, :]

        def rope(z, o_ref):
            for h in range(n_heads):
                cols = slice(h * HEAD_DIM, (h + 1) * HEAD_DIM)
                zh = z[:, cols]
                o_ref[rows, cols] = zh * cos + pltpu.roll(zh, HEAD_DIM // 2, axis=1) * sin

        y_ref[rows, :] = _dot(xn, col(0)) * jax.nn.sigmoid(_dot(xn, col(1)))
        rope(_dot(xn, col(2)), q_ref)
        rope(_dot(xn, col(3)), k_ref)
        v_ref[rows, :] = _dot(xn, col(4)).astype(BF16)


def _in_proj(h, g, w_in, cos, sin, *, tm, seq, sub):
    m, d = h.shape
    width = w_in.shape[1] // 5
    row = lambda i: (i, 0)
    table = pl.BlockSpec((tm, HEAD_DIM), lambda i: (i % (seq // tm), 0))
    return pl.pallas_call(
        functools.partial(_in_proj_kernel, sub=sub),
        grid=(m // tm,),
        in_specs=[pl.BlockSpec((tm, d), row), _resident((1, d)), _resident(w_in.shape), table, table],
        out_specs=[pl.BlockSpec((tm, width), row)] * 4,
        out_shape=[
            jax.ShapeDtypeStruct((m, width), F32),
            jax.ShapeDtypeStruct((m, width), F32),
            jax.ShapeDtypeStruct((m, width), F32),
            jax.ShapeDtypeStruct((m, width), BF16),
        ],
        compiler_params=_params("parallel"),
        name="in_proj",
    )(h, g, w_in, cos, sin)


def _conv_kernel(y_ref, w_ref, b_ref, ln_g_ref, ln_b_ref, o_ref, win_ref, shift_ref, w8_ref, conv_ref,
                 *, rows):
    t = pl.program_id(1)
    ts, c = y_ref.shape

    @pl.when(t == 0)
    def _():
        win_ref[0:CONV_PAD, :] = jnp.zeros((CONV_PAD, c), F32)
        for tap in range(CONV_KERNEL):
            w8_ref[tap] = jnp.broadcast_to(w_ref[tap:tap + 1, :], (SUBLANES, c))

    @pl.when(t > 0)
    def _():
        win_ref[0:CONV_PAD, :] = win_ref[ts:ts + CONV_PAD, :]

    win_ref[CONV_PAD:CONV_PAD + ts, :] = y_ref[...]
    span = shift_ref.shape[1]
    for b in range(1, SUBLANES):
        shift_ref[b - 1] = win_ref[b:b + span, :]

    first = CONV_PAD - (CONV_KERNEL - 1)
    n_sub = rows // SUBLANES

    def chunk(ci, carry):
        r0 = pl.multiple_of(ci * rows, rows)
        accs = [jnp.broadcast_to(b_ref[...], (SUBLANES, c))] * n_sub
        for tap in range(CONV_KERNEL):
            off, b = divmod(first + tap, SUBLANES)
            src = win_ref if b == 0 else shift_ref.at[b - 1]
            wt = w8_ref[tap]
            for i in range(n_sub):
                start = pl.multiple_of(r0 + (off + i) * SUBLANES, SUBLANES)
                accs[i] = accs[i] + src[pl.ds(start, SUBLANES), :] * wt
        for i in range(n_sub):
            conv_ref[pl.ds(pl.multiple_of(r0 + i * SUBLANES, SUBLANES), SUBLANES), :] = accs[i]
        return carry

    lax.fori_loop(0, ts // rows, chunk, 0)

    acc = conv_ref[...]
    mu = jnp.mean(acc, axis=-1, keepdims=True)
    xc = acc - mu
    var = jnp.mean(xc * xc, axis=-1, keepdims=True)
    yn = xc * lax.rsqrt(var + LN_EPS) * ln_g_ref[...] + ln_b_ref[...]
    o_ref[...] = (yn * jax.nn.sigmoid(yn)).astype(o_ref.dtype)


def _conv_group(y, w_dw, b_dw, ln_g, ln_b, *, ts, seq, rows):
    m, c = y.shape
    nt = seq // ts
    const = lambda b, t: (0, 0)
    return pl.pallas_call(
        functools.partial(_conv_kernel, rows=rows),
        grid=(m // seq, nt),
        in_specs=[
            pl.BlockSpec((ts, c), lambda b, t: (b * nt + t, 0)),
            pl.BlockSpec((CONV_KERNEL, c), const),
            pl.BlockSpec((1, c), const),
            pl.BlockSpec((1, c), const),
            pl.BlockSpec((1, c), const),
        ],
        out_specs=pl.BlockSpec((ts, c), lambda b, t: (b * nt + t, 0)),
        out_shape=jax.ShapeDtypeStruct((m, c), BF16),
        scratch_shapes=[
            pltpu.VMEM((CONV_PAD + ts, c), F32),
            pltpu.VMEM((SUBLANES - 1, CONV_PAD + ts - SUBLANES, c), F32),
            pltpu.VMEM((CONV_KERNEL, SUBLANES, c), F32),
            pltpu.VMEM((ts, c), F32),
        ],
        compiler_params=_params("parallel", "arbitrary"),
        name="conv_group",
    )(y, w_dw, b_dw, ln_g, ln_b)


def _moba_kernel(q_ref, k_ref, v_ref, o_ref, kb_ref, vt_ref, kmean_ref, p_ref):
    bs = MOBA_BLOCK
    nb = q_ref.shape[0] // bs
    c = HEAD_DIM ** -0.5 * math.log2(math.e)

    for j in range(nb):
        blk_rows = slice(j * bs, (j + 1) * bs)
        kj = k_ref[blk_rows, :]
        kb_ref[blk_rows, :] = kj.astype(BF16)
        kmean_ref[j:j + 1, :] = jnp.mean(kj, axis=0, keepdims=True)
        vt_ref[:, blk_rows] = v_ref[blk_rows, :].astype(F32).T.astype(BF16)

    kmean = kmean_ref[...]
    blk = lax.broadcasted_iota(jnp.int32, (nb, bs), 0)
    causal = (lax.broadcasted_iota(jnp.int32, (bs, bs), 0)
              <= lax.broadcasted_iota(jnp.int32, (bs, bs), 1))

    def scores(qi):
        q = q_ref[qi * bs:(qi + 1) * bs, :]
        s = lax.dot_general(kb_ref[0:(qi + 1) * bs, :], q.astype(BF16), _NT,
                            preferred_element_type=F32) * c
        keep = None
        if qi > MOBA_TOPK:
            gate = lax.dot_general(kmean, q, _NT, precision=lax.Precision.HIGHEST,
                                   preferred_element_type=F32)
            beaten_by = jnp.zeros((nb, bs), jnp.int32)
            for i in range(qi):
                gi = gate[i:i + 1, :]
                beats = (gi > gate) | ((gi == gate) & (i < blk))
                beaten_by = beaten_by + beats.astype(jnp.int32)
            keep = jnp.where((blk < qi) & (beaten_by < MOBA_TOPK), 1.0, 0.0)
        return s, keep

    def softmax(qi, s, keep):
        blocks = []
        for j in range(qi + 1):
            sj = s[j * bs:(j + 1) * bs, :]
            if j == qi:
                sj = jnp.where(causal, sj, NEG_INF)
            elif keep is not None:
                sj = jnp.where(keep[j:j + 1, :] > 0.5, sj, NEG_INF)
            blocks.append(sj)
        m = functools.reduce(jnp.maximum, [jnp.max(sj, axis=0, keepdims=True) for sj in blocks])
        l = jnp.zeros((1, bs), F32)
        for j, sj in enumerate(blocks):
            p = jnp.exp2(sj - m)
            l = l + jnp.sum(p, axis=0, keepdims=True)
            p_ref[qi % 2, j * bs:(j + 1) * bs, :] = p.astype(BF16)
        return l

    def weighted_values(qi, l):
        nk = (qi + 1) * bs
        acc = _dot(vt_ref[:, 0:nk], p_ref[qi % 2, 0:nk, :])
        o_ref[qi * bs:(qi + 1) * bs, :] = (acc / l).T.astype(o_ref.dtype)

    order = list(range(nb - 1, -1, -1))
    scored, summed = {}, {}
    for step in range(nb + 2):
        if step < nb:
            scored[step] = scores(order[step])
        if 0 <= step - 1 < nb:
            summed[step - 1] = softmax(order[step - 1], *scored.pop(step - 1))
        if 0 <= step - 2 < nb:
            weighted_values(order[step - 2], summed.pop(step - 2))


def _moba(q, k, v, *, seq):
    m, width = q.shape
    n_heads = width // HEAD_DIM
    spec = pl.BlockSpec((seq, HEAD_DIM), lambda b, h: (b, h))
    return pl.pallas_call(
        _moba_kernel,
        grid=(m // seq, n_heads),
        in_specs=[spec, spec, spec],
        out_specs=spec,
        out_shape=jax.ShapeDtypeStruct((m, width), BF16),
        scratch_shapes=[
            pltpu.VMEM((seq, HEAD_DIM), BF16),
            pltpu.VMEM((HEAD_DIM, seq), BF16),
            pltpu.VMEM((seq // MOBA_BLOCK, HEAD_DIM), F32),
            pltpu.VMEM((2, seq, MOBA_BLOCK), BF16),
        ],
        compiler_params=_params("parallel", "parallel"),
        name="moba",
    )(q, k, v)


def _norm_matmul_kernel(x_ref, g_ref, w_ref, o_ref):
    xn = _rms(x_ref[...], g_ref[...]).astype(BF16)
    o_ref[...] = _dot(xn, w_ref[...]).astype(o_ref.dtype)


def _norm_matmul(x, g, w, *, tm, out_dtype):
    m, d = x.shape
    n = w.shape[1]
    return pl.pallas_call(
        _norm_matmul_kernel,
        grid=(m // tm,),
        in_specs=[pl.BlockSpec((tm, d), lambda i: (i, 0)), _resident((1, d)), _resident((d, n))],
        out_specs=pl.BlockSpec((tm, n), lambda i: (i, 0)),
        out_shape=jax.ShapeDtypeStruct((m, n), out_dtype),
        compiler_params=_params("parallel"),
        name="mem_kv",
    )(x, g, w)


def _mix_out_kernel(conv_ref, attn_ref, h_ref, w_out_ref, mix_g_ref,
                    xpre_g_ref, wq_ref, kv_ref, wo_ref, xpost_g_ref, o_ref, *, sub):
    cw = conv_ref.shape[1]
    width = wq_ref.shape[1]
    scale = HEAD_DIM ** -0.5
    n_sub = h_ref.shape[0] // sub
    tile = lambda r: slice(r * sub, (r + 1) * sub)

    def mix(r):
        return _dot(conv_ref[tile(r), :], w_out_ref[0:cw, :]) + _dot(attn_ref[tile(r), :], w_out_ref[cw:, :])

    def query(r, y):
        h = h_ref[tile(r), :] + _rms(y, mix_g_ref[...])
        return h, _dot(_rms(h, xpre_g_ref[...]).astype(BF16), wq_ref[...])

    def attend(r, hq):
        h, q = hq
        heads = []
        for hd in range(N_XATTN_HEADS):
            cols = slice(hd * HEAD_DIM, (hd + 1) * HEAD_DIM)
            vcols = slice(width + hd * HEAD_DIM, width + (hd + 1) * HEAD_DIM)
            s = lax.dot_general(q[:, cols].astype(BF16), kv_ref[:, cols], _NT,
                                preferred_element_type=F32) * scale
            e = jnp.exp(s - jnp.max(s, axis=-1, keepdims=True))
            p = e / jnp.sum(e, axis=-1, keepdims=True)
            heads.append(_dot(p.astype(BF16), kv_ref[:, vcols]))
        return h, jnp.concatenate(heads, axis=1).astype(BF16)

    def project(r, ho):
        h, o = ho
        o_ref[tile(r), :] = h + _rms(_dot(o, wo_ref[...]), xpost_g_ref[...])

    stages = [lambda r, _: mix(r), query, attend, project]
    carried = {}
    for step in range(n_sub + len(stages) - 1):
        for k, stage in enumerate(stages):
            r = step - k
            if 0 <= r < n_sub:
                carried[r] = stage(r, carried.get(r))


def _mix_out(conv, attn, h, w_out, mix_g, xpre_g, w_q, kv, w_o, xpost_g, *, tm, seq, sub):
    m, d = h.shape
    mem_len = kv.shape[0] // (m // seq)
    row = lambda i: (i, 0)
    return pl.pallas_call(
        functools.partial(_mix_out_kernel, sub=sub),
        grid=(m // tm,),
        in_specs=[
            pl.BlockSpec((tm, conv.shape[1]), row),
            pl.BlockSpec((tm, attn.shape[1]), row),
            pl.BlockSpec((tm, d), row),
            _resident(w_out.shape),
            _resident((1, d)),
            _resident((1, d)),
            _resident(w_q.shape),
            pl.BlockSpec((mem_len, kv.shape[1]), lambda i: (i // (seq // tm), 0)),
            _resident(w_o.shape),
            _resident((1, d)),
        ],
        out_specs=pl.BlockSpec((tm, d), row),
        out_shape=jax.ShapeDtypeStruct((m, d), F32),
        compiler_params=_params("parallel"),
        name="mix_out",
    )(conv, attn, h, w_out, mix_g, xpre_g, w_q, kv, w_o, xpost_g)


def _rope_tables(seq):
    inv_freq = 1.0 / (ROPE_THETA ** (jnp.arange(0, HEAD_DIM, 2, dtype=F32) / HEAD_DIM))
    ang = jnp.arange(seq, dtype=F32)[:, None] * inv_freq[None, :]
    cos, sin = jnp.cos(ang), jnp.sin(ang)
    return jnp.concatenate([cos, cos], axis=-1), jnp.concatenate([-sin, sin], axis=-1)


def kernel(x, mem, ffn1_pre_g, ffn1_w_gu, ffn1_w_down, ffn1_post_g, mix_pre_g, w_in, conv_w_dw, conv_b_dw, conv_ln_g, conv_ln_b, w_out, mix_post_g, xattn_pre_g, mem_g, xattn_w_q, xattn_w_kv, xattn_w_o, xattn_post_g, ffn2_pre_g, ffn2_w_gu, ffn2_w_down, ffn2_post_g):
    batch, seq, d = x.shape
    mem_len = mem.shape[1]
    cos, sin = _rope_tables(seq)
    row = lambda a, l: a[l][None, :]
    h = x.reshape(batch * seq, d)
    mem2d = mem.reshape(batch * mem_len, d)

    for l in range(ffn1_w_gu.shape[0]):
        h = _ffn(h, row(ffn1_pre_g, l), ffn1_w_gu[l].astype(BF16), ffn1_w_down[l].astype(BF16),
                 row(ffn1_post_g, l), tm=512, tf=512)

        y, q, k, v = _in_proj(h, row(mix_pre_g, l), w_in[l].astype(BF16), cos, sin,
                              tm=512, seq=seq, sub=256)
        conv = _conv_group(y, conv_w_dw[l], row(conv_b_dw, l), row(conv_ln_g, l), row(conv_ln_b, l),
                           ts=256, seq=seq, rows=32)
        attn = _moba(q, k, v, seq=seq)

        kv = _norm_matmul(mem2d, row(mem_g, l), xattn_w_kv[l].astype(BF16), tm=256, out_dtype=BF16)
        h = _mix_out(conv, attn, h, w_out[l].astype(BF16), row(mix_post_g, l), row(xattn_pre_g, l),
                     xattn_w_q[l].astype(BF16), kv, xattn_w_o[l].astype(BF16), row(xattn_post_g, l),
                     tm=512, seq=seq, sub=256)

        h = _ffn(h, row(ffn2_pre_g, l), ffn2_w_gu[l].astype(BF16), ffn2_w_down[l].astype(BF16),
                 row(ffn2_post_g, l), tm=512, tf=512)
    return h.reshape(batch, seq, d)
```

```python
import functools
import math

import jax
import jax.numpy as jnp
from jax import lax
from jax.experimental import pallas as pl
from jax.experimental.pallas import tpu as pltpu

F32 = jnp.float32
BF16 = jnp.bfloat16

HEAD_DIM = 128
CONV_KERNEL = 31
MOBA_BLOCK = 256
MOBA_TOPK = 3
N_XATTN_HEADS = 4
ROPE_THETA = 10000.0
RMS_EPS = 1e-6
LN_EPS = 1e-5
FFN_RES_SCALE = 0.5
NEG_INF = -1e30

V7X_VMEM_BYTES = 64 * 1024 * 1024
VMEM_LIMIT_BYTES = V7X_VMEM_BYTES - 3 * 1024 * 1024
SUBLANES = 8
CONV_PAD = 32

_NT = (((1,), (1,)), ((), ()))


def _params(*semantics):
    return pltpu.CompilerParams(dimension_semantics=semantics, vmem_limit_bytes=VMEM_LIMIT_BYTES)


def _resident(shape):
    return pl.BlockSpec(shape, lambda *_: (0,) * len(shape), pipeline_mode=pl.Buffered(1))


def _rms(x, g):
    return x * lax.rsqrt(jnp.mean(x * x, axis=-1, keepdims=True) + RMS_EPS) * g


def _dot(a, b):
    return jnp.dot(a, b, preferred_element_type=F32)


def _ffn_kernel(x_ref, pre_g_ref, wg_ref, wu_ref, wd_ref, post_g_ref, o_ref, xn_ref, *, sub):
    j = pl.program_id(1)
    last_j = pl.num_programs(1) - 1
    tiles = [slice(r * sub, (r + 1) * sub) for r in range(o_ref.shape[0] // sub)]
    tf = wd_ref.shape[0]
    chunks = [slice(n * tf, (n + 1) * tf) for n in range(o_ref.shape[1] // tf)]

    def body(first, last):
        for rows in tiles:
            if first:
                xn = _rms(x_ref[rows, :], pre_g_ref[...]).astype(BF16)
                xn_ref[rows, :] = xn
            else:
                xn = xn_ref[rows, :]
            gate = _dot(xn, wg_ref[...])
            up = _dot(xn, wu_ref[...])
            act = (gate * jax.nn.sigmoid(gate) * up).astype(BF16)
            for cols in chunks:
                part = _dot(act, wd_ref[:, cols])
                if first:
                    o_ref[rows, cols] = part
                else:
                    o_ref[rows, cols] += part
            if last:
                o_ref[rows, :] = x_ref[rows, :] + FFN_RES_SCALE * _rms(o_ref[rows, :], post_g_ref[...])

    pl.when(j == 0)(lambda: body(True, False))
    pl.when((j > 0) & (j < last_j))(lambda: body(False, False))
    pl.when(j == last_j)(lambda: body(False, True))


def _ffn(h, pre_g, w_gu, w_down, post_g, *, tm, tf, sub):
    m, d = h.shape
    d_ff = w_down.shape[0]
    nf = d_ff // tf
    return pl.pallas_call(
        functools.partial(_ffn_kernel, sub=sub),
        grid=(m // tm, nf),
        in_specs=[
            pl.BlockSpec((tm, d), lambda i, j: (i, 0)),
            pl.BlockSpec((1, d), lambda i, j: (0, 0)),
            pl.BlockSpec((d, tf), lambda i, j: (0, j)),
            pl.BlockSpec((d, tf), lambda i, j: (0, j + nf)),
            pl.BlockSpec((tf, d), lambda i, j: (j, 0)),
            pl.BlockSpec((1, d), lambda i, j: (0, 0)),
        ],
        out_specs=pl.BlockSpec((tm, d), lambda i, j: (i, 0)),
        out_shape=jax.ShapeDtypeStruct((m, d), F32),
        scratch_shapes=[pltpu.VMEM((tm, d), BF16)],
        compiler_params=_params("parallel", "arbitrary"),
        name="ffn",
    )(h, pre_g, w_gu, w_gu, w_down, post_g)


def _in_proj_kernel(x_ref, g_ref, w_ref, cos_ref, sin_ref, y_ref, q_ref, k_ref, v_ref, *, sub):
    width = y_ref.shape[1]
    n_heads = width // HEAD_DIM
    col = lambda n: w_ref[:, n * width:(n + 1) * width]

    for r in range(x_ref.shape[0] // sub):
        rows = slice(r * sub, (r + 1) * sub)
        xn = _rms(x_ref[rows, :], g_ref[...]).astype(BF16)
        cos, sin = cos_ref[rows, :], sin_ref[rows, :]

        def rope(z, o_ref):
            for h in range(n_heads):
                cols = slice(h * HEAD_DIM, (h + 1) * HEAD_DIM)
                zh = z[:, cols]
                o_ref[rows, cols] = zh * cos + pltpu.roll(zh, HEAD_DIM // 2, axis=1) * sin

        y_ref[rows, :] = _dot(xn, col(0)) * jax.nn.sigmoid(_dot(xn, col(1)))
        rope(_dot(xn, col(2)), q_ref)
        rope(_dot(xn, col(3)), k_ref)
        v_ref[rows, :] = _dot(xn, col(4)).astype(BF16)


def _in_proj(h, g, w_in, cos, sin, *, tm, seq, sub):
    m, d = h.shape
    width = w_in.shape[1] // 5
    row = lambda i: (i, 0)
    table = pl.BlockSpec((tm, HEAD_DIM), lambda i: (i % (seq // tm), 0))
    return pl.pallas_call(
        functools.partial(_in_proj_kernel, sub=sub),
        grid=(m // tm,),
        in_specs=[pl.BlockSpec((tm, d), row), _resident((1, d)), _resident(w_in.shape), table, table],
        out_specs=[pl.BlockSpec((tm, width), row)] * 4,
        out_shape=[
            jax.ShapeDtypeStruct((m, width), F32),
            jax.ShapeDtypeStruct((m, width), F32),
            jax.ShapeDtypeStruct((m, width), F32),
            jax.ShapeDtypeStruct((m, width), BF16),
        ],
        compiler_params=_params("parallel"),
        name="in_proj",
    )(h, g, w_in, cos, sin)


def _conv_kernel(y_ref, w_ref, b_ref, ln_g_ref, ln_b_ref, o_ref, win_ref, shift_ref, w8_ref, conv_ref,
                 *, rows):
    t = pl.program_id(1)
    ts, c = y_ref.shape

    @pl.when(t == 0)
    def _():
        win_ref[0:CONV_PAD, :] = jnp.zeros((CONV_PAD, c), F32)
        for tap in range(CONV_KERNEL):
            w8_ref[tap] = jnp.broadcast_to(w_ref[tap:tap + 1, :], (SUBLANES, c))

    @pl.when(t > 0)
    def _():
        win_ref[0:CONV_PAD, :] = win_ref[ts:ts + CONV_PAD, :]

    win_ref[CONV_PAD:CONV_PAD + ts, :] = y_ref[...]
    span = shift_ref.shape[1]
    for b in range(1, SUBLANES):
        shift_ref[b - 1] = win_ref[b:b + span, :]

    first = CONV_PAD - (CONV_KERNEL - 1)
    n_sub = rows // SUBLANES

    def chunk(ci, carry):
        r0 = pl.multiple_of(ci * rows, rows)
        accs = [jnp.broadcast_to(b_ref[...], (SUBLANES, c))] * n_sub
        for tap in range(CONV_KERNEL):
            off, b = divmod(first + tap, SUBLANES)
            src = win_ref if b == 0 else shift_ref.at[b - 1]
            wt = w8_ref[tap]
            for i in range(n_sub):
                start = pl.multiple_of(r0 + (off + i) * SUBLANES, SUBLANES)
                accs[i] = accs[i] + src[pl.ds(start, SUBLANES), :] * wt
        for i in range(n_sub):
            conv_ref[pl.ds(pl.multiple_of(r0 + i * SUBLANES, SUBLANES), SUBLANES), :] = accs[i]
        return carry

    lax.fori_loop(0, ts // rows, chunk, 0)

    acc = conv_ref[...]
    mu = jnp.mean(acc, axis=-1, keepdims=True)
    xc = acc - mu
    var = jnp.mean(xc * xc, axis=-1, keepdims=True)
    yn = xc * lax.rsqrt(var + LN_EPS) * ln_g_ref[...] + ln_b_ref[...]
    o_ref[...] = (yn * jax.nn.sigmoid(yn)).astype(o_ref.dtype)


def _conv_group(y, w_dw, b_dw, ln_g, ln_b, *, ts, seq, rows):
    m, c = y.shape
    nt = seq // ts
    const = lambda b, t: (0, 0)
    return pl.pallas_call(
        functools.partial(_conv_kernel, rows=rows),
        grid=(m // seq, nt),
        in_specs=[
            pl.BlockSpec((ts, c), lambda b, t: (b * nt + t, 0)),
            pl.BlockSpec((CONV_KERNEL, c), const),
            pl.BlockSpec((1, c), const),
            pl.BlockSpec((1, c), const),
            pl.BlockSpec((1, c), const),
        ],
        out_specs=pl.BlockSpec((ts, c), lambda b, t: (b * nt + t, 0)),
        out_shape=jax.ShapeDtypeStruct((m, c), BF16),
        scratch_shapes=[
            pltpu.VMEM((CONV_PAD + ts, c), F32),
            pltpu.VMEM((SUBLANES - 1, CONV_PAD + ts - SUBLANES, c), F32),
            pltpu.VMEM((CONV_KERNEL, SUBLANES, c), F32),
            pltpu.VMEM((ts, c), F32),
        ],
        compiler_params=_params("parallel", "arbitrary"),
        name="conv_group",
    )(y, w_dw, b_dw, ln_g, ln_b)


def _moba_kernel(q_ref, k_ref, v_ref, o_ref, kb_ref, vt_ref, kmean_ref, p_ref):
    bs = MOBA_BLOCK
    nb = q_ref.shape[0] // bs
    c = HEAD_DIM ** -0.5 * math.log2(math.e)

    for j in range(nb):
        blk_rows = slice(j * bs, (j + 1) * bs)
        kj = k_ref[blk_rows, :]
        kb_ref[blk_rows, :] = kj.astype(BF16)
        kmean_ref[j:j + 1, :] = jnp.mean(kj, axis=0, keepdims=True)
        vt_ref[:, blk_rows] = v_ref[blk_rows, :].astype(F32).T.astype(BF16)

    kmean = kmean_ref[...]
    blk = lax.broadcasted_iota(jnp.int32, (nb, bs), 0)
    causal = (lax.broadcasted_iota(jnp.int32, (bs, bs), 0)
              <= lax.broadcasted_iota(jnp.int32, (bs, bs), 1))

    def scores(qi):
        q = q_ref[qi * bs:(qi + 1) * bs, :]
        s = lax.dot_general(kb_ref[0:(qi + 1) * bs, :], q.astype(BF16), _NT,
                            preferred_element_type=F32) * c
        keep = None
        if qi > MOBA_TOPK:
            gate = lax.dot_general(kmean, q, _NT, precision=lax.Precision.HIGHEST,
                                   preferred_element_type=F32)
            beaten_by = jnp.zeros((nb, bs), jnp.int32)
            for i in range(qi):
                gi = gate[i:i + 1, :]
                beats = (gi > gate) | ((gi == gate) & (i < blk))
                beaten_by = beaten_by + beats.astype(jnp.int32)
            keep = jnp.where((blk < qi) & (beaten_by < MOBA_TOPK), 1.0, 0.0)
        return s, keep

    def softmax(qi, s, keep):
        blocks = []
        for j in range(qi + 1):
            sj = s[j * bs:(j + 1) * bs, :]
            if j == qi:
                sj = jnp.where(causal, sj, NEG_INF)
            elif keep is not None:
                sj = jnp.where(keep[j:j + 1, :] > 0.5, sj, NEG_INF)
            blocks.append(sj)
        m = functools.reduce(jnp.maximum, [jnp.max(sj, axis=0, keepdims=True) for sj in blocks])
        l = jnp.zeros((1, bs), F32)
        for j, sj in enumerate(blocks):
            p = jnp.exp2(sj - m)
            l = l + jnp.sum(p, axis=0, keepdims=True)
            p_ref[qi % 2, j * bs:(j + 1) * bs, :] = p.astype(BF16)
        return l

    def weighted_values(qi, l):
        nk = (qi + 1) * bs
        acc = _dot(vt_ref[:, 0:nk], p_ref[qi % 2, 0:nk, :])
        o_ref[qi * bs:(qi + 1) * bs, :] = (acc / l).T.astype(o_ref.dtype)

    order = list(range(nb - 1, -1, -1))
    scored, summed = {}, {}
    for step in range(nb + 2):
        if step < nb:
            scored[step] = scores(order[step])
        if 0 <= step - 1 < nb:
            summed[step - 1] = softmax(order[step - 1], *scored.pop(step - 1))
        if 0 <= step - 2 < nb:
            weighted_values(order[step - 2], summed.pop(step - 2))


def _moba(q, k, v, *, seq):
    m, width = q.shape
    n_heads = width // HEAD_DIM
    spec = pl.BlockSpec((seq, HEAD_DIM), lambda b, h: (b, h))
    return pl.pallas_call(
        _moba_kernel,
        grid=(m // seq, n_heads),
        in_specs=[spec, spec, spec],
        out_specs=spec,
        out_shape=jax.ShapeDtypeStruct((m, width), BF16),
        scratch_shapes=[
            pltpu.VMEM((seq, HEAD_DIM), BF16),
            pltpu.VMEM((HEAD_DIM, seq), BF16),
            pltpu.VMEM((seq // MOBA_BLOCK, HEAD_DIM), F32),
            pltpu.VMEM((2, seq, MOBA_BLOCK), BF16),
        ],
        compiler_params=_params("parallel", "parallel"),
        name="moba",
    )(q, k, v)


def _norm_matmul_kernel(x_ref, g_ref, w_ref, o_ref):
    xn = _rms(x_ref[...], g_ref[...]).astype(BF16)
    o_ref[...] = _dot(xn, w_ref[...]).astype(o_ref.dtype)


def _norm_matmul(x, g, w, *, tm, out_dtype):
    m, d = x.shape
    n = w.shape[1]
    return pl.pallas_call(
        _norm_matmul_kernel,
        grid=(m // tm,),
        in_specs=[pl.BlockSpec((tm, d), lambda i: (i, 0)), _resident((1, d)), _resident((d, n))],
        out_specs=pl.BlockSpec((tm, n), lambda i: (i, 0)),
        out_shape=jax.ShapeDtypeStruct((m, n), out_dtype),
        compiler_params=_params("parallel"),
        name="mem_kv",
    )(x, g, w)


def _mix_out_kernel(conv_ref, attn_ref, h_ref, w_out_ref, mix_g_ref,
                    xpre_g_ref, wq_ref, kv_ref, wo_ref, xpost_g_ref, o_ref, *, sub):
    cw = conv_ref.shape[1]
    width = wq_ref.shape[1]
    scale = HEAD_DIM ** -0.5
    n_sub = h_ref.shape[0] // sub
    tile = lambda r: slice(r * sub, (r + 1) * sub)

    def mix(r):
        return _dot(conv_ref[tile(r), :], w_out_ref[0:cw, :]) + _dot(attn_ref[tile(r), :], w_out_ref[cw:, :])

    def query(r, y):
        h = h_ref[tile(r), :] + _rms(y, mix_g_ref[...])
        return h, _dot(_rms(h, xpre_g_ref[...]).astype(BF16), wq_ref[...])

    def attend(r, hq):
        h, q = hq
        heads = []
        for hd in range(N_XATTN_HEADS):
            cols = slice(hd * HEAD_DIM, (hd + 1) * HEAD_DIM)
            vcols = slice(width + hd * HEAD_DIM, width + (hd + 1) * HEAD_DIM)
            s = lax.dot_general(q[:, cols].astype(BF16), kv_ref[:, cols], _NT,
                                preferred_element_type=F32) * scale
            e = jnp.exp(s - jnp.max(s, axis=-1, keepdims=True))
            p = e / jnp.sum(e, axis=-1, keepdims=True)
            heads.append(_dot(p.astype(BF16), kv_ref[:, vcols]))
        return h, jnp.concatenate(heads, axis=1).astype(BF16)

    def project(r, ho):
        h, o = ho
        o_ref[tile(r), :] = h + _rms(_dot(o, wo_ref[...]), xpost_g_ref[...])

    stages = [lambda r, _: mix(r), query, attend, project]
    carried = {}
    for step in range(n_sub + len(stages) - 1):
        for k, stage in enumerate(stages):
            r = step - k
            if 0 <= r < n_sub:
                carried[r] = stage(r, carried.get(r))


def _mix_out(conv, attn, h, w_out, mix_g, xpre_g, w_q, kv, w_o, xpost_g, *, tm, seq, sub):
    m, d = h.shape
    mem_len = kv.shape[0] // (m // seq)
    row = lambda i: (i, 0)
    return pl.pallas_call(
        functools.partial(_mix_out_kernel, sub=sub),
        grid=(m // tm,),
        in_specs=[
            pl.BlockSpec((tm, conv.shape[1]), row),
            pl.BlockSpec((tm, attn.shape[1]), row),
            pl.BlockSpec((tm, d), row),
            _resident(w_out.shape),
            _resident((1, d)),
            _resident((1, d)),
            _resident(w_q.shape),
            pl.BlockSpec((mem_len, kv.shape[1]), lambda i: (i // (seq // tm), 0)),
            _resident(w_o.shape),
            _resident((1, d)),
        ],
        out_specs=pl.BlockSpec((tm, d), row),
        out_shape=jax.ShapeDtypeStruct((m, d), F32),
        compiler_params=_params("parallel"),
        name="mix_out",
    )(conv, attn, h, w_out, mix_g, xpre_g, w_q, kv, w_o, xpost_g)


def _rope_tables(seq):
    inv_freq = 1.0 / (ROPE_THETA ** (jnp.arange(0, HEAD_DIM, 2, dtype=F32) / HEAD_DIM))
    ang = jnp.arange(seq, dtype=F32)[:, None] * inv_freq[None, :]
    cos, sin = jnp.cos(ang), jnp.sin(ang)
    return jnp.concatenate([cos, cos], axis=-1), jnp.concatenate([-sin, sin], axis=-1)


def kernel(x, mem, ffn1_pre_g, ffn1_w_gu, ffn1_w_down, ffn1_post_g, mix_pre_g, w_in, conv_w_dw, conv_b_dw, conv_ln_g, conv_ln_b, w_out, mix_post_g, xattn_pre_g, mem_g, xattn_w_q, xattn_w_kv, xattn_w_o, xattn_post_g, ffn2_pre_g, ffn2_w_gu, ffn2_w_down, ffn2_post_g):
    batch, seq, d = x.shape
    mem_len = mem.shape[1]
    cos, sin = _rope_tables(seq)
    row = lambda a, l: a[l][None, :]
    h = x.reshape(batch * seq, d)
    mem2d = mem.reshape(batch * mem_len, d)

    for l in range(ffn1_w_gu.shape[0]):
        h = _ffn(h, row(ffn1_pre_g, l), ffn1_w_gu[l].astype(BF16), ffn1_w_down[l].astype(BF16),
                 row(ffn1_post_g, l), tm=1024, tf=512, sub=512)

        y, q, k, v = _in_proj(h, row(mix_pre_g, l), w_in[l].astype(BF16), cos, sin,
                              tm=512, seq=seq, sub=256)
        conv = _conv_group(y, conv_w_dw[l], row(conv_b_dw, l), row(conv_ln_g, l), row(conv_ln_b, l),
                           ts=256, seq=seq, rows=32)
        attn = _moba(q, k, v, seq=seq)

        kv = _norm_matmul(mem2d, row(mem_g, l), xattn_w_kv[l].astype(BF16), tm=256, out_dtype=BF16)
        h = _mix_out(conv, attn, h, w_out[l].astype(BF16), row(mix_post_g, l), row(xattn_pre_g, l),
                     xattn_w_q[l].astype(BF16), kv, xattn_w_o[l].astype(BF16), row(xattn_post_g, l),
                     tm=512, seq=seq, sub=256)

        h = _ffn(h, row(ffn2_pre_g, l), ffn2_w_gu[l].astype(BF16), ffn2_w_down[l].astype(BF16),
                 row(ffn2_post_g, l), tm=1024, tf=512, sub=512)
    return h.reshape(batch, seq, d)
```

```python
import functools
import math

import jax
import jax.numpy as jnp
from jax import lax
from jax.experimental import pallas as pl
from jax.experimental.pallas import tpu as pltpu

F32 = jnp.float32
BF16 = jnp.bfloat16

HEAD_DIM = 128
CONV_KERNEL = 31
MOBA_BLOCK = 256
MOBA_TOPK = 3
N_XATTN_HEADS = 4
ROPE_THETA = 10000.0
RMS_EPS = 1e-6
LN_EPS = 1e-5
FFN_RES_SCALE = 0.5
NEG_INF = -1e30

V7X_VMEM_BYTES = 64 * 1024 * 1024
VMEM_LIMIT_BYTES = V7X_VMEM_BYTES - 3 * 1024 * 1024
SUBLANES = 8
BF16_SUBLANES = 16
CONV_PAD = 32

_NT = (((1,), (1,)), ((), ()))


def _params(*semantics):
    return pltpu.CompilerParams(dimension_semantics=semantics, vmem_limit_bytes=VMEM_LIMIT_BYTES)


def _resident(shape):
    return pl.BlockSpec(shape, lambda *_: (0,) * len(shape), pipeline_mode=pl.Buffered(1))


def _rms(x, g):
    return x * lax.rsqrt(jnp.mean(x * x, axis=-1, keepdims=True) + RMS_EPS) * g


def _dot(a, b):
    return jnp.dot(a, b, preferred_element_type=F32)


def _ffn_kernel(x_ref, pre_g_ref, wg_ref, wu_ref, wd_ref, post_g_ref, o_ref, xn_ref, *, sub):
    j = pl.program_id(1)
    last_j = pl.num_programs(1) - 1
    tiles = [slice(r * sub, (r + 1) * sub) for r in range(o_ref.shape[0] // sub)]
    tf = wd_ref.shape[0]
    chunks = [slice(n * tf, (n + 1) * tf) for n in range(o_ref.shape[1] // tf)]

    def body(first, last):
        for rows in tiles:
            if first:
                xn = _rms(x_ref[rows, :], pre_g_ref[...]).astype(BF16)
                xn_ref[rows, :] = xn
            else:
                xn = xn_ref[rows, :]
            gate = _dot(xn, wg_ref[...])
            up = _dot(xn, wu_ref[...])
            act = (gate * jax.nn.sigmoid(gate) * up).astype(BF16)
            for cols in chunks:
                part = _dot(act, wd_ref[:, cols])
                if first:
                    o_ref[rows, cols] = part
                else:
                    o_ref[rows, cols] += part
            if last:
                o_ref[rows, :] = x_ref[rows, :] + FFN_RES_SCALE * _rms(o_ref[rows, :], post_g_ref[...])

    pl.when(j == 0)(lambda: body(True, False))
    pl.when((j > 0) & (j < last_j))(lambda: body(False, False))
    pl.when(j == last_j)(lambda: body(False, True))


def _ffn(h, pre_g, w_gu, w_down, post_g, *, tm, tf, sub):
    m, d = h.shape
    d_ff = w_down.shape[0]
    nf = d_ff // tf
    return pl.pallas_call(
        functools.partial(_ffn_kernel, sub=sub),
        grid=(m // tm, nf),
        in_specs=[
            pl.BlockSpec((tm, d), lambda i, j: (i, 0)),
            pl.BlockSpec((1, d), lambda i, j: (0, 0)),
            pl.BlockSpec((d, tf), lambda i, j: (0, j)),
            pl.BlockSpec((d, tf), lambda i, j: (0, j + nf)),
            pl.BlockSpec((tf, d), lambda i, j: (j, 0)),
            pl.BlockSpec((1, d), lambda i, j: (0, 0)),
        ],
        out_specs=pl.BlockSpec((tm, d), lambda i, j: (i, 0)),
        out_shape=jax.ShapeDtypeStruct((m, d), F32),
        scratch_shapes=[pltpu.VMEM((tm, d), BF16)],
        compiler_params=_params("parallel", "arbitrary"),
        name="ffn",
    )(h, pre_g, w_gu, w_gu, w_down, post_g)


def _in_proj_kernel(x_ref, g_ref, w_ref, cos_ref, sin_ref, y_ref, q_ref, k_ref, v_ref, *, sub):
    width = y_ref.shape[1]
    n_heads = width // HEAD_DIM
    col = lambda n: w_ref[:, n * width:(n + 1) * width]

    for r in range(x_ref.shape[0] // sub):
        rows = slice(r * sub, (r + 1) * sub)
        xn = _rms(x_ref[rows, :], g_ref[...]).astype(BF16)
        cos, sin = cos_ref[rows, :], sin_ref[rows, :]

        def rope(z, o_ref):
            for h in range(n_heads):
                cols = slice(h * HEAD_DIM, (h + 1) * HEAD_DIM)
                zh = z[:, cols]
                o_ref[rows, cols] = zh * cos + pltpu.roll(zh, HEAD_DIM // 2, axis=1) * sin

        y_ref[rows, :] = _dot(xn, col(0)) * jax.nn.sigmoid(_dot(xn, col(1)))
        rope(_dot(xn, col(2)), q_ref)
        rope(_dot(xn, col(3)), k_ref)
        v_ref[rows, :] = _dot(xn, col(4)).astype(BF16)


def _in_proj(h, g, w_in, cos, sin, *, tm, seq, sub):
    m, d = h.shape
    width = w_in.shape[1] // 5
    row = lambda i: (i, 0)
    table = pl.BlockSpec((tm, HEAD_DIM), lambda i: (i % (seq // tm), 0))
    return pl.pallas_call(
        functools.partial(_in_proj_kernel, sub=sub),
        grid=(m // tm,),
        in_specs=[pl.BlockSpec((tm, d), row), _resident((1, d)), _resident(w_in.shape), table, table],
        out_specs=[pl.BlockSpec((tm, width), row)] * 4,
        out_shape=[
            jax.ShapeDtypeStruct((m, width), F32),
            jax.ShapeDtypeStruct((m, width), F32),
            jax.ShapeDtypeStruct((m, width), F32),
            jax.ShapeDtypeStruct((m, width), BF16),
        ],
        compiler_params=_params("parallel"),
        name="in_proj",
    )(h, g, w_in, cos, sin)


def _conv_kernel(y_ref, w_ref, b_ref, ln_g_ref, ln_b_ref, o_ref, win_ref, shift_ref, w8_ref, conv_ref,
                 *, rows):
    t = pl.program_id(1)
    ts, c = y_ref.shape

    @pl.when(t == 0)
    def _():
        win_ref[0:CONV_PAD, :] = jnp.zeros((CONV_PAD, c), F32)
        for tap in range(CONV_KERNEL):
            w8_ref[tap] = jnp.broadcast_to(w_ref[tap:tap + 1, :], (SUBLANES, c))

    @pl.when(t > 0)
    def _():
        win_ref[0:CONV_PAD, :] = win_ref[ts:ts + CONV_PAD, :]

    win_ref[CONV_PAD:CONV_PAD + ts, :] = y_ref[...]
    span = shift_ref.shape[1]
    for b in range(1, SUBLANES):
        shift_ref[b - 1] = win_ref[b:b + span, :]

    first = CONV_PAD - (CONV_KERNEL - 1)
    n_sub = rows // SUBLANES

    def chunk(ci, carry):
        r0 = pl.multiple_of(ci * rows, rows)
        accs = [jnp.broadcast_to(b_ref[...], (SUBLANES, c))] * n_sub
        for tap in range(CONV_KERNEL):
            off, b = divmod(first + tap, SUBLANES)
            src = win_ref if b == 0 else shift_ref.at[b - 1]
            wt = w8_ref[tap]
            for i in range(n_sub):
                start = pl.multiple_of(r0 + (off + i) * SUBLANES, SUBLANES)
                accs[i] = accs[i] + src[pl.ds(start, SUBLANES), :] * wt
        for i in range(n_sub):
            conv_ref[pl.ds(pl.multiple_of(r0 + i * SUBLANES, SUBLANES), SUBLANES), :] = accs[i]
        return carry

    lax.fori_loop(0, ts // rows, chunk, 0)

    acc = conv_ref[...]
    mu = jnp.mean(acc, axis=-1, keepdims=True)
    xc = acc - mu
    var = jnp.mean(xc * xc, axis=-1, keepdims=True)
    yn = xc * lax.rsqrt(var + LN_EPS) * ln_g_ref[...] + ln_b_ref[...]
    o_ref[...] = (yn * jax.nn.sigmoid(yn)).astype(o_ref.dtype)


def _conv_group(y, w_dw, b_dw, ln_g, ln_b, *, ts, seq, rows):
    m, c = y.shape
    nt = seq // ts
    const = lambda b, t: (0, 0)
    return pl.pallas_call(
        functools.partial(_conv_kernel, rows=rows),
        grid=(m // seq, nt),
        in_specs=[
            pl.BlockSpec((ts, c), lambda b, t: (b * nt + t, 0)),
            pl.BlockSpec((CONV_KERNEL, c), const),
            pl.BlockSpec((1, c), const),
            pl.BlockSpec((1, c), const),
            pl.BlockSpec((1, c), const),
        ],
        out_specs=pl.BlockSpec((ts, c), lambda b, t: (b * nt + t, 0)),
        out_shape=jax.ShapeDtypeStruct((m, c), BF16),
        scratch_shapes=[
            pltpu.VMEM((CONV_PAD + ts, c), F32),
            pltpu.VMEM((SUBLANES - 1, CONV_PAD + ts - SUBLANES, c), F32),
            pltpu.VMEM((CONV_KERNEL, SUBLANES, c), F32),
            pltpu.VMEM((ts, c), F32),
        ],
        compiler_params=_params("parallel", "arbitrary"),
        name="conv_group",
    )(y, w_dw, b_dw, ln_g, ln_b)


def _moba_kernel(q_ref, k_ref, v_ref, o_ref, kb_ref, vt_ref, kmean_ref, p_ref):
    bs = MOBA_BLOCK
    nb = q_ref.shape[0] // bs
    c = HEAD_DIM ** -0.5 * math.log2(math.e)

    for j in range(nb):
        blk_rows = slice(j * bs, (j + 1) * bs)
        kj = k_ref[blk_rows, :]
        kb_ref[blk_rows, :] = kj.astype(BF16)
        kmean_ref[j:j + 1, :] = jnp.mean(kj, axis=0, keepdims=True)
        vt_ref[0:HEAD_DIM, blk_rows] = v_ref[blk_rows, :].astype(F32).T.astype(BF16)
    vt_ref[HEAD_DIM:, :] = jnp.ones((vt_ref.shape[0] - HEAD_DIM, vt_ref.shape[1]), BF16)

    kmean = kmean_ref[...]
    blk = lax.broadcasted_iota(jnp.int32, (nb, bs), 0)
    causal = (lax.broadcasted_iota(jnp.int32, (bs, bs), 0)
              <= lax.broadcasted_iota(jnp.int32, (bs, bs), 1))

    def scores(qi):
        q = q_ref[qi * bs:(qi + 1) * bs, :]
        s = lax.dot_general(kb_ref[0:(qi + 1) * bs, :], (q * c).astype(BF16), _NT,
                            preferred_element_type=F32)
        keep = None
        if qi > MOBA_TOPK:
            gate = lax.dot_general(kmean, q, _NT, precision=lax.Precision.HIGHEST,
                                   preferred_element_type=F32)
            beaten_by = jnp.zeros((nb, bs), jnp.int32)
            for i in range(qi):
                gi = gate[i:i + 1, :]
                beats = (gi > gate) | ((gi == gate) & (i < blk))
                beaten_by = beaten_by + beats.astype(jnp.int32)
            keep = jnp.where((blk < qi) & (beaten_by < MOBA_TOPK), 1.0, 0.0)
        return s, keep

    def softmax(qi, s, keep):
        blocks = []
        for j in range(qi + 1):
            sj = s[j * bs:(j + 1) * bs, :]
            if j == qi:
                sj = jnp.where(causal, sj, NEG_INF)
            elif keep is not None:
                sj = jnp.where(keep[j:j + 1, :] > 0.5, sj, NEG_INF)
            blocks.append(sj)
        m = functools.reduce(jnp.maximum, [jnp.max(sj, axis=0, keepdims=True) for sj in blocks])
        for j, sj in enumerate(blocks):
            p_ref[qi % 2, j * bs:(j + 1) * bs, :] = jnp.exp2(sj - m).astype(BF16)

    def weighted_values(qi):
        nk = (qi + 1) * bs
        acc = _dot(vt_ref[:, 0:nk], p_ref[qi % 2, 0:nk, :])
        out = acc[0:HEAD_DIM, :] / acc[HEAD_DIM:HEAD_DIM + 1, :]
        o_ref[qi * bs:(qi + 1) * bs, :] = out.T.astype(o_ref.dtype)

    order = list(range(nb - 1, -1, -1))
    scored = {}
    for step in range(nb + 2):
        if step < nb:
            scored[step] = scores(order[step])
        if 0 <= step - 1 < nb:
            softmax(order[step - 1], *scored.pop(step - 1))
        if 0 <= step - 2 < nb:
            weighted_values(order[step - 2])


def _moba(q, k, v, *, seq):
    m, width = q.shape
    n_heads = width // HEAD_DIM
    spec = pl.BlockSpec((seq, HEAD_DIM), lambda b, h: (b, h))
    return pl.pallas_call(
        _moba_kernel,
        grid=(m // seq, n_heads),
        in_specs=[spec, spec, spec],
        out_specs=spec,
        out_shape=jax.ShapeDtypeStruct((m, width), BF16),
        scratch_shapes=[
            pltpu.VMEM((seq, HEAD_DIM), BF16),
            pltpu.VMEM((HEAD_DIM + BF16_SUBLANES, seq), BF16),
            pltpu.VMEM((seq // MOBA_BLOCK, HEAD_DIM), F32),
            pltpu.VMEM((2, seq, MOBA_BLOCK), BF16),
        ],
        compiler_params=_params("parallel", "parallel"),
        name="moba",
    )(q, k, v)


def _norm_matmul_kernel(x_ref, g_ref, w_ref, o_ref):
    xn = _rms(x_ref[...], g_ref[...]).astype(BF16)
    o_ref[...] = _dot(xn, w_ref[...]).astype(o_ref.dtype)


def _norm_matmul(x, g, w, *, tm, out_dtype):
    m, d = x.shape
    n = w.shape[1]
    return pl.pallas_call(
        _norm_matmul_kernel,
        grid=(m // tm,),
        in_specs=[pl.BlockSpec((tm, d), lambda i: (i, 0)), _resident((1, d)), _resident((d, n))],
        out_specs=pl.BlockSpec((tm, n), lambda i: (i, 0)),
        out_shape=jax.ShapeDtypeStruct((m, n), out_dtype),
        compiler_params=_params("parallel"),
        name="mem_kv",
    )(x, g, w)


def _mix_out_kernel(conv_ref, attn_ref, h_ref, w_out_ref, mix_g_ref,
                    xpre_g_ref, wq_ref, kv_ref, wo_ref, xpost_g_ref, o_ref, *, sub):
    cw = conv_ref.shape[1]
    width = wq_ref.shape[1]
    scale = HEAD_DIM ** -0.5
    n_sub = h_ref.shape[0] // sub
    tile = lambda r: slice(r * sub, (r + 1) * sub)

    def mix(r):
        return _dot(conv_ref[tile(r), :], w_out_ref[0:cw, :]) + _dot(attn_ref[tile(r), :], w_out_ref[cw:, :])

    def query(r, y):
        h = h_ref[tile(r), :] + _rms(y, mix_g_ref[...])
        return h, _dot(_rms(h, xpre_g_ref[...]).astype(BF16), wq_ref[...])

    def attend(r, hq):
        h, q = hq
        heads = []
        for hd in range(N_XATTN_HEADS):
            cols = slice(hd * HEAD_DIM, (hd + 1) * HEAD_DIM)
            vcols = slice(width + hd * HEAD_DIM, width + (hd + 1) * HEAD_DIM)
            s = lax.dot_general(q[:, cols].astype(BF16), kv_ref[:, cols], _NT,
                                preferred_element_type=F32) * scale
            e = jnp.exp(s - jnp.max(s, axis=-1, keepdims=True))
            p = e / jnp.sum(e, axis=-1, keepdims=True)
            heads.append(_dot(p.astype(BF16), kv_ref[:, vcols]))
        return h, jnp.concatenate(heads, axis=1).astype(BF16)

    def project(r, ho):
        h, o = ho
        o_ref[tile(r), :] = h + _rms(_dot(o, wo_ref[...]), xpost_g_ref[...])

    stages = [lambda r, _: mix(r), query, attend, project]
    carried = {}
    for step in range(n_sub + len(stages) - 1):
        for k, stage in enumerate(stages):
            r = step - k
            if 0 <= r < n_sub:
                carried[r] = stage(r, carried.get(r))


def _mix_out(conv, attn, h, w_out, mix_g, xpre_g, w_q, kv, w_o, xpost_g, *, tm, seq, sub):
    m, d = h.shape
    mem_len = kv.shape[0] // (m // seq)
    row = lambda i: (i, 0)
    return pl.pallas_call(
        functools.partial(_mix_out_kernel, sub=sub),
        grid=(m // tm,),
        in_specs=[
            pl.BlockSpec((tm, conv.shape[1]), row),
            pl.BlockSpec((tm, attn.shape[1]), row),
            pl.BlockSpec((tm, d), row),
            _resident(w_out.shape),
            _resident((1, d)),
            _resident((1, d)),
            _resident(w_q.shape),
            pl.BlockSpec((mem_len, kv.shape[1]), lambda i: (i // (seq // tm), 0)),
            _resident(w_o.shape),
            _resident((1, d)),
        ],
        out_specs=pl.BlockSpec((tm, d), row),
        out_shape=jax.ShapeDtypeStruct((m, d), F32),
        compiler_params=_params("parallel"),
        name="mix_out",
    )(conv, attn, h, w_out, mix_g, xpre_g, w_q, kv, w_o, xpost_g)


def _rope_tables(seq):
    inv_freq = 1.0 / (ROPE_THETA ** (jnp.arange(0, HEAD_DIM, 2, dtype=F32) / HEAD_DIM))
    ang = jnp.arange(seq, dtype=F32)[:, None] * inv_freq[None, :]
    cos, sin = jnp.cos(ang), jnp.sin(ang)
    return jnp.concatenate([cos, cos], axis=-1), jnp.concatenate([-sin, sin], axis=-1)


def kernel(x, mem, ffn1_pre_g, ffn1_w_gu, ffn1_w_down, ffn1_post_g, mix_pre_g, w_in, conv_w_dw, conv_b_dw, conv_ln_g, conv_ln_b, w_out, mix_post_g, xattn_pre_g, mem_g, xattn_w_q, xattn_w_kv, xattn_w_o, xattn_post_g, ffn2_pre_g, ffn2_w_gu, ffn2_w_down, ffn2_post_g):
    batch, seq, d = x.shape
    mem_len = mem.shape[1]
    cos, sin = _rope_tables(seq)
    row = lambda a, l: a[l][None, :]
    h = x.reshape(batch * seq, d)
    mem2d = mem.reshape(batch * mem_len, d)

    for l in range(ffn1_w_gu.shape[0]):
        h = _ffn(h, row(ffn1_pre_g, l), ffn1_w_gu[l].astype(BF16), ffn1_w_down[l].astype(BF16),
                 row(ffn1_post_g, l), tm=1024, tf=512, sub=512)

        y, q, k, v = _in_proj(h, row(mix_pre_g, l), w_in[l].astype(BF16), cos, sin,
                              tm=512, seq=seq, sub=256)
        conv = _conv_group(y, conv_w_dw[l], row(conv_b_dw, l), row(conv_ln_g, l), row(conv_ln_b, l),
                           ts=512, seq=seq, rows=32)
        attn = _moba(q, k, v, seq=seq)

        kv = _norm_matmul(mem2d, row(mem_g, l), xattn_w_kv[l].astype(BF16), tm=256, out_dtype=BF16)
        h = _mix_out(conv, attn, h, w_out[l].astype(BF16), row(mix_post_g, l), row(xattn_pre_g, l),
                     xattn_w_q[l].astype(BF16), kv, xattn_w_o[l].astype(BF16), row(xattn_post_g, l),
                     tm=512, seq=seq, sub=256)

        h = _ffn(h, row(ffn2_pre_g, l), ffn2_w_gu[l].astype(BF16), ffn2_w_down[l].astype(BF16),
                 row(ffn2_post_g, l), tm=1024, tf=512, sub=512)
    return h.reshape(batch, seq, d)
```

```python
import functools
import math

import jax
import jax.numpy as jnp
from jax import lax
from jax.experimental import pallas as pl
from jax.experimental.pallas import tpu as pltpu

F32 = jnp.float32
BF16 = jnp.bfloat16

HEAD_DIM = 128
CONV_KERNEL = 31
MOBA_BLOCK = 256
MOBA_TOPK = 3
N_XATTN_HEADS = 4
ROPE_THETA = 10000.0
RMS_EPS = 1e-6
LN_EPS = 1e-5
FFN_RES_SCALE = 0.5
NEG_INF = -1e30

V7X_VMEM_BYTES = 64 * 1024 * 1024
VMEM_LIMIT_BYTES = V7X_VMEM_BYTES - 3 * 1024 * 1024
SUBLANES = 8
BF16_SUBLANES = 16
CONV_PAD = 32

_NT = (((1,), (1,)), ((), ()))

FFN_ROWS, FFN_SUB_ROWS, FFN_COLS = 1024, 512, 512
IN_PROJ_ROWS, IN_PROJ_SUB_ROWS = 512, 256
CONV_ROWS, CONV_CHUNK_ROWS = 512, 32
MIX_OUT_ROWS, MIX_OUT_SUB_ROWS = 512, 256
MEM_KV_ROWS = 256


def _params(*semantics):
    return pltpu.CompilerParams(dimension_semantics=semantics, vmem_limit_bytes=VMEM_LIMIT_BYTES)


def _resident(shape):
    return pl.BlockSpec(shape, lambda *_: (0,) * len(shape), pipeline_mode=pl.Buffered(1))


def _rms(x, g):
    return x * lax.rsqrt(jnp.mean(x * x, axis=-1, keepdims=True) + RMS_EPS) * g


def _dot(a, b):
    return jnp.dot(a, b, preferred_element_type=F32)


def _ffn_kernel(x_ref, pre_g_ref, wgu_ref, wd_ref, post_g_ref, o_ref, xn_ref, *, sub):
    j = pl.program_id(1)
    last_j = pl.num_programs(1) - 1
    tiles = [slice(r * sub, (r + 1) * sub) for r in range(o_ref.shape[0] // sub)]
    tf = wd_ref.shape[0]
    chunks = [slice(n * tf, (n + 1) * tf) for n in range(o_ref.shape[1] // tf)]

    def body(first, last):
        for rows in tiles:
            if first:
                xn = _rms(x_ref[rows, :], pre_g_ref[...]).astype(BF16)
                xn_ref[rows, :] = xn
            else:
                xn = xn_ref[rows, :]
            gate_up = _dot(xn, wgu_ref[...])
            gate, up = gate_up[:, 0:tf], gate_up[:, tf:]
            act = (gate * jax.nn.sigmoid(gate) * up).astype(BF16)
            for cols in chunks:
                part = _dot(act, wd_ref[:, cols])
                if first:
                    o_ref[rows, cols] = part
                else:
                    o_ref[rows, cols] += part
            if last:
                o_ref[rows, :] = x_ref[rows, :] + FFN_RES_SCALE * _rms(o_ref[rows, :], post_g_ref[...])

    pl.when(j == 0)(lambda: body(True, False))
    pl.when((j > 0) & (j < last_j))(lambda: body(False, False))
    pl.when(j == last_j)(lambda: body(False, True))


def _ffn(h, pre_g, w_gu, w_down, post_g, *, tm, tf, sub):
    m, d = h.shape
    d_ff = w_down.shape[0]
    nf = d_ff // tf
    return pl.pallas_call(
        functools.partial(_ffn_kernel, sub=sub),
        grid=(m // tm, nf),
        in_specs=[
            pl.BlockSpec((tm, d), lambda i, j: (i, 0)),
            pl.BlockSpec((1, d), lambda i, j: (0, 0)),
            pl.BlockSpec((d, 2 * tf), lambda i, j: (0, j)),
            pl.BlockSpec((tf, d), lambda i, j: (j, 0)),
            pl.BlockSpec((1, d), lambda i, j: (0, 0)),
        ],
        out_specs=pl.BlockSpec((tm, d), lambda i, j: (i, 0)),
        out_shape=jax.ShapeDtypeStruct((m, d), F32),
        scratch_shapes=[pltpu.VMEM((tm, d), BF16)],
        compiler_params=_params("parallel", "arbitrary"),
        name="ffn",
    )(h, pre_g, w_gu, w_down, post_g)


def _in_proj_kernel(x_ref, g_ref, w_ref, cos_ref, sin_ref, y_ref, q_ref, k_ref, v_ref, *, sub):
    width = y_ref.shape[1]
    n_heads = width // HEAD_DIM
    col = lambda n: w_ref[:, n * width:(n + 1) * width]

    for r in range(x_ref.shape[0] // sub):
        rows = slice(r * sub, (r + 1) * sub)
        xn = _rms(x_ref[rows, :], g_ref[...]).astype(BF16)
        cos, sin = cos_ref[rows, :], sin_ref[rows, :]

        def rope(z, o_ref):
            for h in range(n_heads):
                cols = slice(h * HEAD_DIM, (h + 1) * HEAD_DIM)
                zh = z[:, cols]
                o_ref[rows, cols] = zh * cos + pltpu.roll(zh, HEAD_DIM // 2, axis=1) * sin

        y_ref[rows, :] = _dot(xn, col(0)) * jax.nn.sigmoid(_dot(xn, col(1)))
        rope(_dot(xn, col(2)), q_ref)
        rope(_dot(xn, col(3)), k_ref)
        v_ref[rows, :] = _dot(xn, col(4)).astype(BF16)


def _in_proj(h, g, w_in, cos, sin, *, tm, seq, sub):
    m, d = h.shape
    width = w_in.shape[1] // 5
    row = lambda i: (i, 0)
    table = pl.BlockSpec((tm, HEAD_DIM), lambda i: (i % (seq // tm), 0))
    return pl.pallas_call(
        functools.partial(_in_proj_kernel, sub=sub),
        grid=(m // tm,),
        in_specs=[pl.BlockSpec((tm, d), row), _resident((1, d)), _resident(w_in.shape), table, table],
        out_specs=[pl.BlockSpec((tm, width), row)] * 4,
        out_shape=[
            jax.ShapeDtypeStruct((m, width), F32),
            jax.ShapeDtypeStruct((m, width), F32),
            jax.ShapeDtypeStruct((m, width), F32),
            jax.ShapeDtypeStruct((m, width), BF16),
        ],
        compiler_params=_params("parallel"),
        name="in_proj",
    )(h, g, w_in, cos, sin)


def _conv_kernel(y_ref, w_ref, b_ref, ln_g_ref, ln_b_ref, o_ref, win_ref, shift_ref, w8_ref, conv_ref,
                 *, rows):
    t = pl.program_id(1)
    ts, c = y_ref.shape

    @pl.when(t == 0)
    def _():
        win_ref[0:CONV_PAD, :] = jnp.zeros((CONV_PAD, c), F32)
        for tap in range(CONV_KERNEL):
            w8_ref[tap] = jnp.broadcast_to(w_ref[tap:tap + 1, :], (SUBLANES, c))

    @pl.when(t > 0)
    def _():
        win_ref[0:CONV_PAD, :] = win_ref[ts:ts + CONV_PAD, :]

    win_ref[CONV_PAD:CONV_PAD + ts, :] = y_ref[...]
    span = shift_ref.shape[1]
    for b in range(1, SUBLANES):
        shift_ref[b - 1] = win_ref[b:b + span, :]

    first = CONV_PAD - (CONV_KERNEL - 1)
    n_sub = rows // SUBLANES

    def chunk(ci, carry):
        r0 = pl.multiple_of(ci * rows, rows)
        accs = [jnp.broadcast_to(b_ref[...], (SUBLANES, c))] * n_sub
        for tap in range(CONV_KERNEL):
            off, b = divmod(first + tap, SUBLANES)
            src = win_ref if b == 0 else shift_ref.at[b - 1]
            wt = w8_ref[tap]
            for i in range(n_sub):
                start = pl.multiple_of(r0 + (off + i) * SUBLANES, SUBLANES)
                accs[i] = accs[i] + src[pl.ds(start, SUBLANES), :] * wt
        for i in range(n_sub):
            conv_ref[pl.ds(pl.multiple_of(r0 + i * SUBLANES, SUBLANES), SUBLANES), :] = accs[i]
        return carry

    lax.fori_loop(0, ts // rows, chunk, 0)

    acc = conv_ref[...]
    mu = jnp.mean(acc, axis=-1, keepdims=True)
    xc = acc - mu
    var = jnp.mean(xc * xc, axis=-1, keepdims=True)
    yn = xc * lax.rsqrt(var + LN_EPS) * ln_g_ref[...] + ln_b_ref[...]
    o_ref[...] = (yn * jax.nn.sigmoid(yn)).astype(o_ref.dtype)


def _conv_group(y, w_dw, b_dw, ln_g, ln_b, *, ts, seq, rows):
    m, c = y.shape
    nt = seq // ts
    const = lambda b, t: (0, 0)
    return pl.pallas_call(
        functools.partial(_conv_kernel, rows=rows),
        grid=(m // seq, nt),
        in_specs=[
            pl.BlockSpec((ts, c), lambda b, t: (b * nt + t, 0)),
            pl.BlockSpec((CONV_KERNEL, c), const),
            pl.BlockSpec((1, c), const),
            pl.BlockSpec((1, c), const),
            pl.BlockSpec((1, c), const),
        ],
        out_specs=pl.BlockSpec((ts, c), lambda b, t: (b * nt + t, 0)),
        out_shape=jax.ShapeDtypeStruct((m, c), BF16),
        scratch_shapes=[
            pltpu.VMEM((CONV_PAD + ts, c), F32),
            pltpu.VMEM((SUBLANES - 1, CONV_PAD + ts - SUBLANES, c), F32),
            pltpu.VMEM((CONV_KERNEL, SUBLANES, c), F32),
            pltpu.VMEM((ts, c), F32),
        ],
        compiler_params=_params("parallel", "arbitrary"),
        name="conv_group",
    )(y, w_dw, b_dw, ln_g, ln_b)


def _moba_kernel(q_ref, k_ref, v_ref, o_ref, kb_ref, vt_ref, kmean_ref, p_ref):
    bs = MOBA_BLOCK
    nb = q_ref.shape[0] // bs
    c = HEAD_DIM ** -0.5 * math.log2(math.e)

    for j in range(nb):
        blk_rows = slice(j * bs, (j + 1) * bs)
        kj = k_ref[blk_rows, :]
        kb_ref[blk_rows, :] = kj.astype(BF16)
        kmean_ref[j:j + 1, :] = jnp.mean(kj, axis=0, keepdims=True)
        vt_ref[0:HEAD_DIM, blk_rows] = v_ref[blk_rows, :].astype(F32).T.astype(BF16)
    vt_ref[HEAD_DIM:, :] = jnp.ones((vt_ref.shape[0] - HEAD_DIM, vt_ref.shape[1]), BF16)

    kmean = kmean_ref[...]
    blk = lax.broadcasted_iota(jnp.int32, (nb, bs), 0)
    causal = (lax.broadcasted_iota(jnp.int32, (bs, bs), 0)
              <= lax.broadcasted_iota(jnp.int32, (bs, bs), 1))

    def scores(qi):
        q = q_ref[qi * bs:(qi + 1) * bs, :]
        s = lax.dot_general(kb_ref[0:(qi + 1) * bs, :], (q * c).astype(BF16), _NT,
                            preferred_element_type=F32)
        keep = None
        if qi > MOBA_TOPK:
            gate = lax.dot_general(kmean, q, _NT, precision=lax.Precision.HIGHEST,
                                   preferred_element_type=F32)
            beaten_by = jnp.zeros((nb, bs), jnp.int32)
            for i in range(qi):
                gi = gate[i:i + 1, :]
                beats = (gi > gate) | ((gi == gate) & (i < blk))
                beaten_by = beaten_by + beats.astype(jnp.int32)
            keep = jnp.where((blk < qi) & (beaten_by < MOBA_TOPK), 1.0, 0.0)
        return s, keep

    def softmax(qi, s, keep):
        blocks = []
        for j in range(qi + 1):
            sj = s[j * bs:(j + 1) * bs, :]
            if j == qi:
                sj = jnp.where(causal, sj, NEG_INF)
            elif keep is not None:
                sj = jnp.where(keep[j:j + 1, :] > 0.5, sj, NEG_INF)
            blocks.append(sj)
        m = functools.reduce(jnp.maximum, [jnp.max(sj, axis=0, keepdims=True) for sj in blocks])
        for j, sj in enumerate(blocks):
            p_ref[qi % 2, j * bs:(j + 1) * bs, :] = jnp.exp2(sj - m).astype(BF16)

    def weighted_values(qi):
        nk = (qi + 1) * bs
        acc = _dot(vt_ref[:, 0:nk], p_ref[qi % 2, 0:nk, :])
        out = acc[0:HEAD_DIM, :] / acc[HEAD_DIM:HEAD_DIM + 1, :]
        o_ref[qi * bs:(qi + 1) * bs, :] = out.T.astype(o_ref.dtype)

    order = list(range(nb - 1, -1, -1))
    scored = {}
    for step in range(nb + 2):
        if step < nb:
            scored[step] = scores(order[step])
        if 0 <= step - 1 < nb:
            softmax(order[step - 1], *scored.pop(step - 1))
        if 0 <= step - 2 < nb:
            weighted_values(order[step - 2])


def _moba(q, k, v, *, seq):
    m, width = q.shape
    n_heads = width // HEAD_DIM
    spec = pl.BlockSpec((seq, HEAD_DIM), lambda b, h: (b, h))
    return pl.pallas_call(
        _moba_kernel,
        grid=(m // seq, n_heads),
        in_specs=[spec, spec, spec],
        out_specs=spec,
        out_shape=jax.ShapeDtypeStruct((m, width), BF16),
        scratch_shapes=[
            pltpu.VMEM((seq, HEAD_DIM), BF16),
            pltpu.VMEM((HEAD_DIM + BF16_SUBLANES, seq), BF16),
            pltpu.VMEM((seq // MOBA_BLOCK, HEAD_DIM), F32),
            pltpu.VMEM((2, seq, MOBA_BLOCK), BF16),
        ],
        compiler_params=_params("parallel", "parallel"),
        name="moba",
    )(q, k, v)


def _norm_matmul_kernel(x_ref, g_ref, w_ref, o_ref):
    xn = _rms(x_ref[...], g_ref[...]).astype(BF16)
    o_ref[...] = _dot(xn, w_ref[...]).astype(o_ref.dtype)


def _norm_matmul(x, g, w, *, tm, out_dtype):
    m, d = x.shape
    n = w.shape[1]
    return pl.pallas_call(
        _norm_matmul_kernel,
        grid=(m // tm,),
        in_specs=[pl.BlockSpec((tm, d), lambda i: (i, 0)), _resident((1, d)), _resident((d, n))],
        out_specs=pl.BlockSpec((tm, n), lambda i: (i, 0)),
        out_shape=jax.ShapeDtypeStruct((m, n), out_dtype),
        compiler_params=_params("parallel"),
        name="mem_kv",
    )(x, g, w)


def _mix_out_kernel(conv_ref, attn_ref, h_ref, w_out_ref, mix_g_ref,
                    xpre_g_ref, wq_ref, kv_ref, wo_ref, xpost_g_ref, o_ref, *, sub):
    cw = conv_ref.shape[1]
    width = wq_ref.shape[1]
    scale = HEAD_DIM ** -0.5
    n_sub = h_ref.shape[0] // sub
    tile = lambda r: slice(r * sub, (r + 1) * sub)

    def mix(r):
        return _dot(conv_ref[tile(r), :], w_out_ref[0:cw, :]) + _dot(attn_ref[tile(r), :], w_out_ref[cw:, :])

    def query(r, y):
        h = h_ref[tile(r), :] + _rms(y, mix_g_ref[...])
        return h, _dot(_rms(h, xpre_g_ref[...]).astype(BF16), wq_ref[...])

    def attend(r, hq):
        h, q = hq
        heads = []
        for hd in range(N_XATTN_HEADS):
            cols = slice(hd * HEAD_DIM, (hd + 1) * HEAD_DIM)
            vcols = slice(width + hd * HEAD_DIM, width + (hd + 1) * HEAD_DIM)
            s = lax.dot_general(q[:, cols].astype(BF16), kv_ref[:, cols], _NT,
                                preferred_element_type=F32) * scale
            e = jnp.exp(s - jnp.max(s, axis=-1, keepdims=True))
            p = e / jnp.sum(e, axis=-1, keepdims=True)
            heads.append(_dot(p.astype(BF16), kv_ref[:, vcols]))
        return h, jnp.concatenate(heads, axis=1).astype(BF16)

    def project(r, ho):
        h, o = ho
        o_ref[tile(r), :] = h + _rms(_dot(o, wo_ref[...]), xpost_g_ref[...])

    stages = [lambda r, _: mix(r), query, attend, project]
    carried = {}
    for step in range(n_sub + len(stages) - 1):
        for k, stage in enumerate(stages):
            r = step - k
            if 0 <= r < n_sub:
                carried[r] = stage(r, carried.get(r))


def _mix_out(conv, attn, h, w_out, mix_g, xpre_g, w_q, kv, w_o, xpost_g, *, tm, seq, sub):
    m, d = h.shape
    mem_len = kv.shape[0] // (m // seq)
    row = lambda i: (i, 0)
    return pl.pallas_call(
        functools.partial(_mix_out_kernel, sub=sub),
        grid=(m // tm,),
        in_specs=[
            pl.BlockSpec((tm, conv.shape[1]), row),
            pl.BlockSpec((tm, attn.shape[1]), row),
            pl.BlockSpec((tm, d), row),
            _resident(w_out.shape),
            _resident((1, d)),
            _resident((1, d)),
            _resident(w_q.shape),
            pl.BlockSpec((mem_len, kv.shape[1]), lambda i: (i // (seq // tm), 0)),
            _resident(w_o.shape),
            _resident((1, d)),
        ],
        out_specs=pl.BlockSpec((tm, d), row),
        out_shape=jax.ShapeDtypeStruct((m, d), F32),
        compiler_params=_params("parallel"),
        name="mix_out",
    )(conv, attn, h, w_out, mix_g, xpre_g, w_q, kv, w_o, xpost_g)


def _pair_gate_up(w_gu, tf):
    d, two_ff = w_gu.shape
    nf = two_ff // (2 * tf)
    return w_gu.astype(BF16).reshape(d, 2, nf, tf).transpose(0, 2, 1, 3).reshape(d, two_ff)


def _rope_tables(seq):
    inv_freq = 1.0 / (ROPE_THETA ** (jnp.arange(0, HEAD_DIM, 2, dtype=F32) / HEAD_DIM))
    ang = jnp.arange(seq, dtype=F32)[:, None] * inv_freq[None, :]
    cos, sin = jnp.cos(ang), jnp.sin(ang)
    return jnp.concatenate([cos, cos], axis=-1), jnp.concatenate([-sin, sin], axis=-1)


def kernel(x, mem, ffn1_pre_g, ffn1_w_gu, ffn1_w_down, ffn1_post_g, mix_pre_g, w_in, conv_w_dw, conv_b_dw, conv_ln_g, conv_ln_b, w_out, mix_post_g, xattn_pre_g, mem_g, xattn_w_q, xattn_w_kv, xattn_w_o, xattn_post_g, ffn2_pre_g, ffn2_w_gu, ffn2_w_down, ffn2_post_g):
    batch, seq, d = x.shape
    mem_len = mem.shape[1]
    cos, sin = _rope_tables(seq)
    row = lambda a, l: a[l][None, :]
    h = x.reshape(batch * seq, d)
    mem2d = mem.reshape(batch * mem_len, d)

    for l in range(ffn1_w_gu.shape[0]):
        h = _ffn(h, row(ffn1_pre_g, l), _pair_gate_up(ffn1_w_gu[l], FFN_COLS), ffn1_w_down[l].astype(BF16),
                 row(ffn1_post_g, l), tm=FFN_ROWS, tf=FFN_COLS, sub=FFN_SUB_ROWS)

        y, q, k, v = _in_proj(h, row(mix_pre_g, l), w_in[l].astype(BF16), cos, sin,
                              tm=IN_PROJ_ROWS, seq=seq, sub=IN_PROJ_SUB_ROWS)
        conv = _conv_group(y, conv_w_dw[l], row(conv_b_dw, l), row(conv_ln_g, l), row(conv_ln_b, l),
                           ts=CONV_ROWS, seq=seq, rows=CONV_CHUNK_ROWS)
        attn = _moba(q, k, v, seq=seq)

        kv = _norm_matmul(mem2d, row(mem_g, l), xattn_w_kv[l].astype(BF16), tm=MEM_KV_ROWS,
                          out_dtype=BF16)
        h = _mix_out(conv, attn, h, w_out[l].astype(BF16), row(mix_post_g, l), row(xattn_pre_g, l),
                     xattn_w_q[l].astype(BF16), kv, xattn_w_o[l].astype(BF16), row(xattn_post_g, l),
                     tm=MIX_OUT_ROWS, seq=seq, sub=MIX_OUT_SUB_ROWS)

        h = _ffn(h, row(ffn2_pre_g, l), _pair_gate_up(ffn2_w_gu[l], FFN_COLS), ffn2_w_down[l].astype(BF16),
                 row(ffn2_post_g, l), tm=FFN_ROWS, tf=FFN_COLS, sub=FFN_SUB_ROWS)
    return h.reshape(batch, seq, d)
```

```python
import functools
import math

import jax
import jax.numpy as jnp
from jax import lax
from jax.experimental import pallas as pl
from jax.experimental.pallas import tpu as pltpu

F32 = jnp.float32
BF16 = jnp.bfloat16

HEAD_DIM = 128
CONV_KERNEL = 31
MOBA_BLOCK = 256
MOBA_TOPK = 3
N_XATTN_HEADS = 4
ROPE_THETA = 10000.0
RMS_EPS = 1e-6
LN_EPS = 1e-5
FFN_RES_SCALE = 0.5
NEG_INF = -1e30

V7X_VMEM_BYTES = 64 * 1024 * 1024
VMEM_LIMIT_BYTES = V7X_VMEM_BYTES - 3 * 1024 * 1024
SUBLANES = 8
BF16_SUBLANES = 16
CONV_PAD = 32

_NT = (((1,), (1,)), ((), ()))

FFN_ROWS, FFN_SUB_ROWS, FFN_COLS = 1024, 512, 512
IN_PROJ_ROWS, IN_PROJ_SUB_ROWS = 512, 256
CONV_ROWS, CONV_CHUNK_ROWS = 512, 32
MIX_OUT_ROWS, MIX_OUT_SUB_ROWS = 512, 256
MEM_KV_ROWS = 256
MOBA_HEADS_PER_STEP = 4


def _params(*semantics):
    return pltpu.CompilerParams(dimension_semantics=semantics, vmem_limit_bytes=VMEM_LIMIT_BYTES)


def _resident(shape):
    return pl.BlockSpec(shape, lambda *_: (0,) * len(shape), pipeline_mode=pl.Buffered(1))


def _rms(x, g):
    return x * lax.rsqrt(jnp.mean(x * x, axis=-1, keepdims=True) + RMS_EPS) * g


def _dot(a, b):
    return jnp.dot(a, b, preferred_element_type=F32)


def _ffn_kernel(x_ref, pre_g_ref, wg_ref, wu_ref, wd_ref, post_g_ref, o_ref, xn_ref, *, sub):
    j = pl.program_id(1)
    last_j = pl.num_programs(1) - 1
    tiles = [slice(r * sub, (r + 1) * sub) for r in range(o_ref.shape[0] // sub)]
    tf = wd_ref.shape[0]
    chunks = [slice(n * tf, (n + 1) * tf) for n in range(o_ref.shape[1] // tf)]

    def body(first, last):
        for rows in tiles:
            if first:
                xn = _rms(x_ref[rows, :], pre_g_ref[...]).astype(BF16)
                xn_ref[rows, :] = xn
            else:
                xn = xn_ref[rows, :]
            gate = _dot(xn, wg_ref[...])
            up = _dot(xn, wu_ref[...])
            act = (gate * jax.nn.sigmoid(gate) * up).astype(BF16)
            for cols in chunks:
                part = _dot(act, wd_ref[:, cols])
                if first:
                    o_ref[rows, cols] = part
                else:
                    o_ref[rows, cols] += part
            if last:
                o_ref[rows, :] = x_ref[rows, :] + FFN_RES_SCALE * _rms(o_ref[rows, :], post_g_ref[...])

    pl.when(j == 0)(lambda: body(True, False))
    pl.when((j > 0) & (j < last_j))(lambda: body(False, False))
    pl.when(j == last_j)(lambda: body(False, True))


def _ffn(h, pre_g, w_gu, w_down, post_g, *, tm, tf, sub):
    m, d = h.shape
    d_ff = w_down.shape[0]
    nf = d_ff // tf
    return pl.pallas_call(
        functools.partial(_ffn_kernel, sub=sub),
        grid=(m // tm, nf),
        in_specs=[
            pl.BlockSpec((tm, d), lambda i, j: (i, 0)),
            pl.BlockSpec((1, d), lambda i, j: (0, 0)),
            pl.BlockSpec((d, tf), lambda i, j: (0, j)),
            pl.BlockSpec((d, tf), lambda i, j: (0, j + nf)),
            pl.BlockSpec((tf, d), lambda i, j: (j, 0)),
            pl.BlockSpec((1, d), lambda i, j: (0, 0)),
        ],
        out_specs=pl.BlockSpec((tm, d), lambda i, j: (i, 0)),
        out_shape=jax.ShapeDtypeStruct((m, d), F32),
        scratch_shapes=[pltpu.VMEM((tm, d), BF16)],
        compiler_params=_params("parallel", "arbitrary"),
        name="ffn",
    )(h, pre_g, w_gu, w_gu, w_down, post_g)


def _in_proj_kernel(x_ref, g_ref, w_ref, cos_ref, sin_ref, y_ref, q_ref, k_ref, v_ref, *, sub):
    width = y_ref.shape[1]
    n_heads = width // HEAD_DIM
    col = lambda n: w_ref[:, n * width:(n + 1) * width]

    for r in range(x_ref.shape[0] // sub):
        rows = slice(r * sub, (r + 1) * sub)
        xn = _rms(x_ref[rows, :], g_ref[...]).astype(BF16)
        cos, sin = cos_ref[rows, :], sin_ref[rows, :]

        def rope(z, o_ref):
            for h in range(n_heads):
                cols = slice(h * HEAD_DIM, (h + 1) * HEAD_DIM)
                zh = z[:, cols]
                o_ref[rows, cols] = zh * cos + pltpu.roll(zh, HEAD_DIM // 2, axis=1) * sin

        y_ref[rows, :] = _dot(xn, col(0)) * jax.nn.sigmoid(_dot(xn, col(1)))
        rope(_dot(xn, col(2)), q_ref)
        rope(_dot(xn, col(3)), k_ref)
        v_ref[rows, :] = _dot(xn, col(4)).astype(BF16)


def _in_proj(h, g, w_in, cos, sin, *, tm, seq, sub):
    m, d = h.shape
    width = w_in.shape[1] // 5
    row = lambda i: (i, 0)
    table = pl.BlockSpec((tm, HEAD_DIM), lambda i: (i % (seq // tm), 0))
    return pl.pallas_call(
        functools.partial(_in_proj_kernel, sub=sub),
        grid=(m // tm,),
        in_specs=[pl.BlockSpec((tm, d), row), _resident((1, d)), _resident(w_in.shape), table, table],
        out_specs=[pl.BlockSpec((tm, width), row)] * 4,
        out_shape=[
            jax.ShapeDtypeStruct((m, width), F32),
            jax.ShapeDtypeStruct((m, width), F32),
            jax.ShapeDtypeStruct((m, width), F32),
            jax.ShapeDtypeStruct((m, width), BF16),
        ],
        compiler_params=_params("parallel"),
        name="in_proj",
    )(h, g, w_in, cos, sin)


def _conv_kernel(y_ref, w_ref, b_ref, ln_g_ref, ln_b_ref, o_ref, win_ref, shift_ref, w8_ref, conv_ref,
                 *, rows):
    t = pl.program_id(1)
    ts, c = y_ref.shape

    @pl.when(t == 0)
    def _():
        win_ref[0:CONV_PAD, :] = jnp.zeros((CONV_PAD, c), F32)
        for tap in range(CONV_KERNEL):
            w8_ref[tap] = jnp.broadcast_to(w_ref[tap:tap + 1, :], (SUBLANES, c))

    @pl.when(t > 0)
    def _():
        win_ref[0:CONV_PAD, :] = win_ref[ts:ts + CONV_PAD, :]

    win_ref[CONV_PAD:CONV_PAD + ts, :] = y_ref[...]
    span = shift_ref.shape[1]
    for b in range(1, SUBLANES):
        shift_ref[b - 1] = win_ref[b:b + span, :]

    first = CONV_PAD - (CONV_KERNEL - 1)
    n_sub = rows // SUBLANES

    def chunk(ci, carry):
        r0 = pl.multiple_of(ci * rows, rows)
        accs = [jnp.broadcast_to(b_ref[...], (SUBLANES, c))] * n_sub
        for tap in range(CONV_KERNEL):
            off, b = divmod(first + tap, SUBLANES)
            src = win_ref if b == 0 else shift_ref.at[b - 1]
            wt = w8_ref[tap]
            for i in range(n_sub):
                start = pl.multiple_of(r0 + (off + i) * SUBLANES, SUBLANES)
                accs[i] = accs[i] + src[pl.ds(start, SUBLANES), :] * wt
        for i in range(n_sub):
            conv_ref[pl.ds(pl.multiple_of(r0 + i * SUBLANES, SUBLANES), SUBLANES), :] = accs[i]
        return carry

    lax.fori_loop(0, ts // rows, chunk, 0)

    acc = conv_ref[...]
    mu = jnp.mean(acc, axis=-1, keepdims=True)
    xc = acc - mu
    var = jnp.mean(xc * xc, axis=-1, keepdims=True)
    yn = xc * lax.rsqrt(var + LN_EPS) * ln_g_ref[...] + ln_b_ref[...]
    o_ref[...] = (yn * jax.nn.sigmoid(yn)).astype(o_ref.dtype)


def _conv_group(y, w_dw, b_dw, ln_g, ln_b, *, ts, seq, rows):
    m, c = y.shape
    nt = seq // ts
    const = lambda b, t: (0, 0)
    return pl.pallas_call(
        functools.partial(_conv_kernel, rows=rows),
        grid=(m // seq, nt),
        in_specs=[
            pl.BlockSpec((ts, c), lambda b, t: (b * nt + t, 0)),
            pl.BlockSpec((CONV_KERNEL, c), const),
            pl.BlockSpec((1, c), const),
            pl.BlockSpec((1, c), const),
            pl.BlockSpec((1, c), const),
        ],
        out_specs=pl.BlockSpec((ts, c), lambda b, t: (b * nt + t, 0)),
        out_shape=jax.ShapeDtypeStruct((m, c), BF16),
        scratch_shapes=[
            pltpu.VMEM((CONV_PAD + ts, c), F32),
            pltpu.VMEM((SUBLANES - 1, CONV_PAD + ts - SUBLANES, c), F32),
            pltpu.VMEM((CONV_KERNEL, SUBLANES, c), F32),
            pltpu.VMEM((ts, c), F32),
        ],
        compiler_params=_params("parallel", "arbitrary"),
        name="conv_group",
    )(y, w_dw, b_dw, ln_g, ln_b)


def _moba_kernel(q_ref, k_ref, v_ref, o_ref, kb_ref, vt_ref, kmean_ref, p_ref):
    bs = MOBA_BLOCK
    nb = q_ref.shape[0] // bs
    n_heads = q_ref.shape[1] // HEAD_DIM
    c = HEAD_DIM ** -0.5 * math.log2(math.e)
    head = lambda h: slice(h * HEAD_DIM, (h + 1) * HEAD_DIM)

    for h in range(n_heads):
        for j in range(nb):
            blk_rows = slice(j * bs, (j + 1) * bs)
            kj = k_ref[blk_rows, head(h)]
            kb_ref[h, blk_rows, :] = kj.astype(BF16)
            kmean_ref[h, j:j + 1, :] = jnp.mean(kj, axis=0, keepdims=True)
            vt_ref[h, 0:HEAD_DIM, blk_rows] = v_ref[blk_rows, head(h)].astype(F32).T.astype(BF16)
        vt_ref[h, HEAD_DIM:, :] = jnp.ones((vt_ref.shape[1] - HEAD_DIM, vt_ref.shape[2]), BF16)

    blk = lax.broadcasted_iota(jnp.int32, (nb, bs), 0)
    causal = (lax.broadcasted_iota(jnp.int32, (bs, bs), 0)
              <= lax.broadcasted_iota(jnp.int32, (bs, bs), 1))

    def scores(h, qi):
        q = q_ref[qi * bs:(qi + 1) * bs, head(h)]
        s = lax.dot_general(kb_ref[h, 0:(qi + 1) * bs, :], (q * c).astype(BF16), _NT,
                            preferred_element_type=F32)
        keep = None
        if qi > MOBA_TOPK:
            gate = lax.dot_general(kmean_ref[h], q, _NT, precision=lax.Precision.HIGHEST,
                                   preferred_element_type=F32)
            beaten_by = jnp.zeros((nb, bs), jnp.int32)
            for i in range(qi):
                gi = gate[i:i + 1, :]
                beats = (gi > gate) | ((gi == gate) & (i < blk))
                beaten_by = beaten_by + beats.astype(jnp.int32)
            keep = jnp.where((blk < qi) & (beaten_by < MOBA_TOPK), 1.0, 0.0)
        return s, keep

    def softmax(slot, qi, s, keep):
        blocks = []
        for j in range(qi + 1):
            sj = s[j * bs:(j + 1) * bs, :]
            if j == qi:
                sj = jnp.where(causal, sj, NEG_INF)
            elif keep is not None:
                sj = jnp.where(keep[j:j + 1, :] > 0.5, sj, NEG_INF)
            blocks.append(sj)
        m = functools.reduce(jnp.maximum, [jnp.max(sj, axis=0, keepdims=True) for sj in blocks])
        for j, sj in enumerate(blocks):
            p_ref[slot, j * bs:(j + 1) * bs, :] = jnp.exp2(sj - m).astype(BF16)

    def weighted_values(slot, h, qi):
        nk = (qi + 1) * bs
        acc = _dot(vt_ref[h, :, 0:nk], p_ref[slot, 0:nk, :])
        out = acc[0:HEAD_DIM, :] / acc[HEAD_DIM:HEAD_DIM + 1, :]
        o_ref[qi * bs:(qi + 1) * bs, head(h)] = out.T.astype(o_ref.dtype)

    items = [(h, qi) for qi in range(nb - 1, -1, -1) for h in range(n_heads)]
    scored = {}
    for step in range(len(items) + 2):
        if step < len(items):
            scored[step] = scores(*items[step])
        if 0 <= step - 1 < len(items):
            softmax((step - 1) % 2, items[step - 1][1], *scored.pop(step - 1))
        if 0 <= step - 2 < len(items):
            weighted_values((step - 2) % 2, *items[step - 2])


def _moba(q, k, v, *, seq, heads_per_step):
    m, width = q.shape
    cols = heads_per_step * HEAD_DIM
    spec = pl.BlockSpec((seq, cols), lambda b, h: (b, h))
    return pl.pallas_call(
        _moba_kernel,
        grid=(m // seq, width // cols),
        in_specs=[spec, spec, spec],
        out_specs=spec,
        out_shape=jax.ShapeDtypeStruct((m, width), BF16),
        scratch_shapes=[
            pltpu.VMEM((heads_per_step, seq, HEAD_DIM), BF16),
            pltpu.VMEM((heads_per_step, HEAD_DIM + BF16_SUBLANES, seq), BF16),
            pltpu.VMEM((heads_per_step, seq // MOBA_BLOCK, HEAD_DIM), F32),
            pltpu.VMEM((2, seq, MOBA_BLOCK), BF16),
        ],
        compiler_params=_params("parallel", "parallel"),
        name="moba",
    )(q, k, v)


def _norm_matmul_kernel(x_ref, g_ref, w_ref, o_ref):
    xn = _rms(x_ref[...], g_ref[...]).astype(BF16)
    o_ref[...] = _dot(xn, w_ref[...]).astype(o_ref.dtype)


def _norm_matmul(x, g, w, *, tm, out_dtype):
    m, d = x.shape
    n = w.shape[1]
    return pl.pallas_call(
        _norm_matmul_kernel,
        grid=(m // tm,),
        in_specs=[pl.BlockSpec((tm, d), lambda i: (i, 0)), _resident((1, d)), _resident((d, n))],
        out_specs=pl.BlockSpec((tm, n), lambda i: (i, 0)),
        out_shape=jax.ShapeDtypeStruct((m, n), out_dtype),
        compiler_params=_params("parallel"),
        name="mem_kv",
    )(x, g, w)


def _mix_out_kernel(conv_ref, attn_ref, h_ref, w_out_ref, mix_g_ref,
                    xpre_g_ref, wq_ref, kv_ref, wo_ref, xpost_g_ref, o_ref, *, sub):
    cw = conv_ref.shape[1]
    width = wq_ref.shape[1]
    scale = HEAD_DIM ** -0.5
    n_sub = h_ref.shape[0] // sub
    tile = lambda r: slice(r * sub, (r + 1) * sub)

    def mix(r):
        return _dot(conv_ref[tile(r), :], w_out_ref[0:cw, :]) + _dot(attn_ref[tile(r), :], w_out_ref[cw:, :])

    def query(r, y):
        h = h_ref[tile(r), :] + _rms(y, mix_g_ref[...])
        return h, _dot(_rms(h, xpre_g_ref[...]).astype(BF16), wq_ref[...])

    def attend(r, hq):
        h, q = hq
        heads = []
        for hd in range(N_XATTN_HEADS):
            cols = slice(hd * HEAD_DIM, (hd + 1) * HEAD_DIM)
            vcols = slice(width + hd * HEAD_DIM, width + (hd + 1) * HEAD_DIM)
            s = lax.dot_general(q[:, cols].astype(BF16), kv_ref[:, cols], _NT,
                                preferred_element_type=F32) * scale
            e = jnp.exp(s - jnp.max(s, axis=-1, keepdims=True))
            p = e / jnp.sum(e, axis=-1, keepdims=True)
            heads.append(_dot(p.astype(BF16), kv_ref[:, vcols]))
        return h, jnp.concatenate(heads, axis=1).astype(BF16)

    def project(r, ho):
        h, o = ho
        o_ref[tile(r), :] = h + _rms(_dot(o, wo_ref[...]), xpost_g_ref[...])

    stages = [lambda r, _: mix(r), query, attend, project]
    carried = {}
    for step in range(n_sub + len(stages) - 1):
        for k, stage in enumerate(stages):
            r = step - k
            if 0 <= r < n_sub:
                carried[r] = stage(r, carried.get(r))


def _mix_out(conv, attn, h, w_out, mix_g, xpre_g, w_q, kv, w_o, xpost_g, *, tm, seq, sub):
    m, d = h.shape
    mem_len = kv.shape[0] // (m // seq)
    row = lambda i: (i, 0)
    return pl.pallas_call(
        functools.partial(_mix_out_kernel, sub=sub),
        grid=(m // tm,),
        in_specs=[
            pl.BlockSpec((tm, conv.shape[1]), row),
            pl.BlockSpec((tm, attn.shape[1]), row),
            pl.BlockSpec((tm, d), row),
            _resident(w_out.shape),
            _resident((1, d)),
            _resident((1, d)),
            _resident(w_q.shape),
            pl.BlockSpec((mem_len, kv.shape[1]), lambda i: (i // (seq // tm), 0)),
            _resident(w_o.shape),
            _resident((1, d)),
        ],
        out_specs=pl.BlockSpec((tm, d), row),
        out_shape=jax.ShapeDtypeStruct((m, d), F32),
        compiler_params=_params("parallel"),
        name="mix_out",
    )(conv, attn, h, w_out, mix_g, xpre_g, w_q, kv, w_o, xpost_g)


def _rope_tables(seq):
    inv_freq = 1.0 / (ROPE_THETA ** (jnp.arange(0, HEAD_DIM, 2, dtype=F32) / HEAD_DIM))
    ang = jnp.arange(seq, dtype=F32)[:, None] * inv_freq[None, :]
    cos, sin = jnp.cos(ang), jnp.sin(ang)
    return jnp.concatenate([cos, cos], axis=-1), jnp.concatenate([-sin, sin], axis=-1)


def kernel(x, mem, ffn1_pre_g, ffn1_w_gu, ffn1_w_down, ffn1_post_g, mix_pre_g, w_in, conv_w_dw, conv_b_dw, conv_ln_g, conv_ln_b, w_out, mix_post_g, xattn_pre_g, mem_g, xattn_w_q, xattn_w_kv, xattn_w_o, xattn_post_g, ffn2_pre_g, ffn2_w_gu, ffn2_w_down, ffn2_post_g):
    batch, seq, d = x.shape
    mem_len = mem.shape[1]
    cos, sin = _rope_tables(seq)
    row = lambda a, l: a[l][None, :]
    h = x.reshape(batch * seq, d)
    mem2d = mem.reshape(batch * mem_len, d)

    for l in range(ffn1_w_gu.shape[0]):
        h = _ffn(h, row(ffn1_pre_g, l), ffn1_w_gu[l].astype(BF16), ffn1_w_down[l].astype(BF16),
                 row(ffn1_post_g, l), tm=FFN_ROWS, tf=FFN_COLS, sub=FFN_SUB_ROWS)

        y, q, k, v = _in_proj(h, row(mix_pre_g, l), w_in[l].astype(BF16), cos, sin,
                              tm=IN_PROJ_ROWS, seq=seq, sub=IN_PROJ_SUB_ROWS)
        conv = _conv_group(y, conv_w_dw[l], row(conv_b_dw, l), row(conv_ln_g, l), row(conv_ln_b, l),
                           ts=CONV_ROWS, seq=seq, rows=CONV_CHUNK_ROWS)
        attn = _moba(q, k, v, seq=seq, heads_per_step=MOBA_HEADS_PER_STEP)

        kv = _norm_matmul(mem2d, row(mem_g, l), xattn_w_kv[l].astype(BF16), tm=MEM_KV_ROWS,
                          out_dtype=BF16)
        h = _mix_out(conv, attn, h, w_out[l].astype(BF16), row(mix_post_g, l), row(xattn_pre_g, l),
                     xattn_w_q[l].astype(BF16), kv, xattn_w_o[l].astype(BF16), row(xattn_post_g, l),
                     tm=MIX_OUT_ROWS, seq=seq, sub=MIX_OUT_SUB_ROWS)

        h = _ffn(h, row(ffn2_pre_g, l), ffn2_w_gu[l].astype(BF16), ffn2_w_down[l].astype(BF16),
                 row(ffn2_post_g, l), tm=FFN_ROWS, tf=FFN_COLS, sub=FFN_SUB_ROWS)
    return h.reshape(batch, seq, d)
```

```python
import functools
import math

import jax
import jax.numpy as jnp
from jax import lax
from jax.experimental import pallas as pl
from jax.experimental.pallas import tpu as pltpu

F32 = jnp.float32
BF16 = jnp.bfloat16

HEAD_DIM = 128
CONV_KERNEL = 31
MOBA_BLOCK = 256
MOBA_TOPK = 3
N_XATTN_HEADS = 4
ROPE_THETA = 10000.0
RMS_EPS = 1e-6
LN_EPS = 1e-5
FFN_RES_SCALE = 0.5
NEG_INF = -1e30

V7X_VMEM_BYTES = 64 * 1024 * 1024
VMEM_LIMIT_BYTES = V7X_VMEM_BYTES - 3 * 1024 * 1024
SUBLANES = 8
BF16_SUBLANES = 16
CONV_PAD = 32

_NT = (((1,), (1,)), ((), ()))

FFN_ROWS, FFN_SUB_ROWS, FFN_COLS = 1024, 512, 512
IN_PROJ_ROWS, IN_PROJ_SUB_ROWS = 512, 256
CONV_ROWS, CONV_CHUNK_ROWS = 512, 32
MIX_OUT_ROWS, MIX_OUT_SUB_ROWS = 512, 256
MEM_KV_ROWS = 256
MOBA_HEADS_PER_STEP = 4


def _params(*semantics):
    return pltpu.CompilerParams(dimension_semantics=semantics, vmem_limit_bytes=VMEM_LIMIT_BYTES)


def _resident(shape):
    return pl.BlockSpec(shape, lambda *_: (0,) * len(shape), pipeline_mode=pl.Buffered(1))


def _rms(x, g):
    return x * lax.rsqrt(jnp.mean(x * x, axis=-1, keepdims=True) + RMS_EPS) * g


def _dot(a, b):
    return jnp.dot(a, b, preferred_element_type=F32)


def _ffn_kernel(x_ref, pre_g_ref, wg_ref, wu_ref, wd_ref, post_g_ref, o_ref, xn_ref, *, sub):
    j = pl.program_id(1)
    last_j = pl.num_programs(1) - 1
    tiles = [slice(r * sub, (r + 1) * sub) for r in range(o_ref.shape[0] // sub)]
    tf = wd_ref.shape[0]
    chunks = [slice(n * tf, (n + 1) * tf) for n in range(o_ref.shape[1] // tf)]

    def body(first, last):
        for rows in tiles:
            if first:
                xn = _rms(x_ref[rows, :], pre_g_ref[...]).astype(BF16)
                xn_ref[rows, :] = xn
            else:
                xn = xn_ref[rows, :]
            gate = _dot(xn, wg_ref[...])
            up = _dot(xn, wu_ref[...])
            act = (gate * jax.nn.sigmoid(gate) * up).astype(BF16)
            for cols in chunks:
                part = _dot(act, wd_ref[:, cols])
                if first:
                    o_ref[rows, cols] = part
                else:
                    o_ref[rows, cols] += part
            if last:
                o_ref[rows, :] = x_ref[rows, :] + FFN_RES_SCALE * _rms(o_ref[rows, :], post_g_ref[...])

    pl.when(j == 0)(lambda: body(True, False))
    pl.when((j > 0) & (j < last_j))(lambda: body(False, False))
    pl.when(j == last_j)(lambda: body(False, True))


def _ffn(h, pre_g, w_gu, w_down, post_g, *, tm, tf, sub):
    m, d = h.shape
    d_ff = w_down.shape[0]
    nf = d_ff // tf
    return pl.pallas_call(
        functools.partial(_ffn_kernel, sub=sub),
        grid=(m // tm, nf),
        in_specs=[
            pl.BlockSpec((tm, d), lambda i, j: (i, 0)),
            pl.BlockSpec((1, d), lambda i, j: (0, 0)),
            pl.BlockSpec((d, tf), lambda i, j: (0, j)),
            pl.BlockSpec((d, tf), lambda i, j: (0, j + nf)),
            pl.BlockSpec((tf, d), lambda i, j: (j, 0)),
            pl.BlockSpec((1, d), lambda i, j: (0, 0)),
        ],
        out_specs=pl.BlockSpec((tm, d), lambda i, j: (i, 0)),
        out_shape=jax.ShapeDtypeStruct((m, d), F32),
        scratch_shapes=[pltpu.VMEM((tm, d), BF16)],
        compiler_params=_params("parallel", "arbitrary"),
        name="ffn",
    )(h, pre_g, w_gu, w_gu, w_down, post_g)


def _in_proj_kernel(x_ref, g_ref, w_ref, cos_ref, sin_ref, y_ref, q_ref, k_ref, v_ref, *, sub):
    width = y_ref.shape[1]
    n_heads = width // HEAD_DIM
    col = lambda n: w_ref[:, n * width:(n + 1) * width]

    for r in range(x_ref.shape[0] // sub):
        rows = slice(r * sub, (r + 1) * sub)
        xn = _rms(x_ref[rows, :], g_ref[...]).astype(BF16)
        cos, sin = cos_ref[rows, :], sin_ref[rows, :]

        def rope(z, o_ref):
            for h in range(n_heads):
                cols = slice(h * HEAD_DIM, (h + 1) * HEAD_DIM)
                zh = z[:, cols]
                o_ref[rows, cols] = zh * cos + pltpu.roll(zh, HEAD_DIM // 2, axis=1) * sin

        y_ref[rows, :] = _dot(xn, col(0)) * jax.nn.sigmoid(_dot(xn, col(1)))
        rope(_dot(xn, col(2)), q_ref)
        rope(_dot(xn, col(3)), k_ref)
        v_ref[rows, :] = _dot(xn, col(4)).astype(BF16)


def _in_proj(h, g, w_in, cos, sin, *, tm, seq, sub):
    m, d = h.shape
    width = w_in.shape[1] // 5
    row = lambda i: (i, 0)
    table = pl.BlockSpec((tm, HEAD_DIM), lambda i: (i % (seq // tm), 0))
    return pl.pallas_call(
        functools.partial(_in_proj_kernel, sub=sub),
        grid=(m // tm,),
        in_specs=[pl.BlockSpec((tm, d), row), _resident((1, d)), _resident(w_in.shape), table, table],
        out_specs=[pl.BlockSpec((tm, width), row)] * 4,
        out_shape=[
            jax.ShapeDtypeStruct((m, width), F32),
            jax.ShapeDtypeStruct((m, width), F32),
            jax.ShapeDtypeStruct((m, width), F32),
            jax.ShapeDtypeStruct((m, width), BF16),
        ],
        compiler_params=_params("parallel"),
        name="in_proj",
    )(h, g, w_in, cos, sin)


def _conv_kernel(y_ref, w_ref, b_ref, ln_g_ref, ln_b_ref, o_ref, win_ref, shift_ref, w8_ref, conv_ref,
                 *, rows):
    t = pl.program_id(1)
    ts, c = y_ref.shape

    @pl.when(t == 0)
    def _():
        win_ref[0:CONV_PAD, :] = jnp.zeros((CONV_PAD, c), F32)
        for tap in range(CONV_KERNEL):
            w8_ref[tap] = jnp.broadcast_to(w_ref[tap:tap + 1, :], (SUBLANES, c))

    @pl.when(t > 0)
    def _():
        win_ref[0:CONV_PAD, :] = win_ref[ts:ts + CONV_PAD, :]

    win_ref[CONV_PAD:CONV_PAD + ts, :] = y_ref[...]
    span = shift_ref.shape[1]
    for b in range(1, SUBLANES):
        shift_ref[b - 1] = win_ref[b:b + span, :]

    first = CONV_PAD - (CONV_KERNEL - 1)
    n_sub = rows // SUBLANES

    def chunk(ci, carry):
        r0 = pl.multiple_of(ci * rows, rows)
        accs = [jnp.broadcast_to(b_ref[...], (SUBLANES, c))] * n_sub
        for tap in range(CONV_KERNEL):
            off, b = divmod(first + tap, SUBLANES)
            src = win_ref if b == 0 else shift_ref.at[b - 1]
            wt = w8_ref[tap]
            for i in range(n_sub):
                start = pl.multiple_of(r0 + (off + i) * SUBLANES, SUBLANES)
                accs[i] = accs[i] + src[pl.ds(start, SUBLANES), :] * wt
        for i in range(n_sub):
            conv_ref[pl.ds(pl.multiple_of(r0 + i * SUBLANES, SUBLANES), SUBLANES), :] = accs[i]
        return carry

    lax.fori_loop(0, ts // rows, chunk, 0, unroll=2)

    acc = conv_ref[...]
    mu = jnp.mean(acc, axis=-1, keepdims=True)
    xc = acc - mu
    var = jnp.mean(xc * xc, axis=-1, keepdims=True)
    yn = xc * lax.rsqrt(var + LN_EPS) * ln_g_ref[...] + ln_b_ref[...]
    o_ref[...] = (yn * jax.nn.sigmoid(yn)).astype(o_ref.dtype)


def _conv_group(y, w_dw, b_dw, ln_g, ln_b, *, ts, seq, rows):
    m, c = y.shape
    nt = seq // ts
    const = lambda b, t: (0, 0)
    return pl.pallas_call(
        functools.partial(_conv_kernel, rows=rows),
        grid=(m // seq, nt),
        in_specs=[
            pl.BlockSpec((ts, c), lambda b, t: (b * nt + t, 0)),
            pl.BlockSpec((CONV_KERNEL, c), const),
            pl.BlockSpec((1, c), const),
            pl.BlockSpec((1, c), const),
            pl.BlockSpec((1, c), const),
        ],
        out_specs=pl.BlockSpec((ts, c), lambda b, t: (b * nt + t, 0)),
        out_shape=jax.ShapeDtypeStruct((m, c), BF16),
        scratch_shapes=[
            pltpu.VMEM((CONV_PAD + ts, c), F32),
            pltpu.VMEM((SUBLANES - 1, CONV_PAD + ts - SUBLANES, c), F32),
            pltpu.VMEM((CONV_KERNEL, SUBLANES, c), F32),
            pltpu.VMEM((ts, c), F32),
        ],
        compiler_params=_params("parallel", "arbitrary"),
        name="conv_group",
    )(y, w_dw, b_dw, ln_g, ln_b)


def _moba_kernel(q_ref, k_ref, v_ref, o_ref, kb_ref, vt_ref, kmean_ref, p_ref):
    bs = MOBA_BLOCK
    nb = q_ref.shape[0] // bs
    n_heads = q_ref.shape[1] // HEAD_DIM
    c = HEAD_DIM ** -0.5 * math.log2(math.e)
    head = lambda h: slice(h * HEAD_DIM, (h + 1) * HEAD_DIM)

    for h in range(n_heads):
        for j in range(nb):
            blk_rows = slice(j * bs, (j + 1) * bs)
            kj = k_ref[blk_rows, head(h)]
            kb_ref[h, blk_rows, :] = kj.astype(BF16)
            kmean_ref[h, j:j + 1, :] = jnp.mean(kj, axis=0, keepdims=True)
            vt_ref[h, 0:HEAD_DIM, blk_rows] = v_ref[blk_rows, head(h)].astype(F32).T.astype(BF16)
        vt_ref[h, HEAD_DIM:, :] = jnp.ones((vt_ref.shape[1] - HEAD_DIM, vt_ref.shape[2]), BF16)

    blk = lax.broadcasted_iota(jnp.int32, (nb, bs), 0)
    causal = (lax.broadcasted_iota(jnp.int32, (bs, bs), 0)
              <= lax.broadcasted_iota(jnp.int32, (bs, bs), 1))

    def scores(h, qi):
        q = q_ref[qi * bs:(qi + 1) * bs, head(h)]
        s = lax.dot_general(kb_ref[h, 0:(qi + 1) * bs, :], (q * c).astype(BF16), _NT,
                            preferred_element_type=F32)
        keep = None
        if qi > MOBA_TOPK:
            gate = lax.dot_general(kmean_ref[h], q, _NT, precision=lax.Precision.HIGHEST,
                                   preferred_element_type=F32)
            beaten_by = jnp.zeros((nb, bs), jnp.int32)
            for i in range(qi):
                gi = gate[i:i + 1, :]
                beats = (gi > gate) | ((gi == gate) & (i < blk))
                beaten_by = beaten_by + beats.astype(jnp.int32)
            keep = jnp.where((blk < qi) & (beaten_by < MOBA_TOPK), 1.0, 0.0)
        return s, keep

    def softmax(slot, qi, s, keep):
        blocks = []
        for j in range(qi + 1):
            sj = s[j * bs:(j + 1) * bs, :]
            if j == qi:
                sj = jnp.where(causal, sj, NEG_INF)
            elif keep is not None:
                sj = jnp.where(keep[j:j + 1, :] > 0.5, sj, NEG_INF)
            blocks.append(sj)
        m = functools.reduce(jnp.maximum, [jnp.max(sj, axis=0, keepdims=True) for sj in blocks])
        for j, sj in enumerate(blocks):
            p_ref[slot, j * bs:(j + 1) * bs, :] = jnp.exp2(sj - m).astype(BF16)

    def weighted_values(slot, h, qi):
        nk = (qi + 1) * bs
        acc = _dot(vt_ref[h, :, 0:nk], p_ref[slot, 0:nk, :])
        out = acc[0:HEAD_DIM, :] / acc[HEAD_DIM:HEAD_DIM + 1, :]
        o_ref[qi * bs:(qi + 1) * bs, head(h)] = out.T.astype(o_ref.dtype)

    items = [(h, qi) for qi in range(nb - 1, -1, -1) for h in range(n_heads)]
    scored = {}
    for step in range(len(items) + 2):
        if step < len(items):
            scored[step] = scores(*items[step])
        if 0 <= step - 1 < len(items):
            softmax((step - 1) % 2, items[step - 1][1], *scored.pop(step - 1))
        if 0 <= step - 2 < len(items):
            weighted_values((step - 2) % 2, *items[step - 2])


def _moba(q, k, v, *, seq, heads_per_step):
    m, width = q.shape
    cols = heads_per_step * HEAD_DIM
    spec = pl.BlockSpec((seq, cols), lambda b, h: (b, h))
    return pl.pallas_call(
        _moba_kernel,
        grid=(m // seq, width // cols),
        in_specs=[spec, spec, spec],
        out_specs=spec,
        out_shape=jax.ShapeDtypeStruct((m, width), BF16),
        scratch_shapes=[
            pltpu.VMEM((heads_per_step, seq, HEAD_DIM), BF16),
            pltpu.VMEM((heads_per_step, HEAD_DIM + BF16_SUBLANES, seq), BF16),
            pltpu.VMEM((heads_per_step, seq // MOBA_BLOCK, HEAD_DIM), F32),
            pltpu.VMEM((2, seq, MOBA_BLOCK), BF16),
        ],
        compiler_params=_params("parallel", "parallel"),
        name="moba",
    )(q, k, v)


def _norm_matmul_kernel(x_ref, g_ref, w_ref, o_ref):
    xn = _rms(x_ref[...], g_ref[...]).astype(BF16)
    o_ref[...] = _dot(xn, w_ref[...]).astype(o_ref.dtype)


def _norm_matmul(x, g, w, *, tm, out_dtype):
    m, d = x.shape
    n = w.shape[1]
    return pl.pallas_call(
        _norm_matmul_kernel,
        grid=(m // tm,),
        in_specs=[pl.BlockSpec((tm, d), lambda i: (i, 0)), _resident((1, d)), _resident((d, n))],
        out_specs=pl.BlockSpec((tm, n), lambda i: (i, 0)),
        out_shape=jax.ShapeDtypeStruct((m, n), out_dtype),
        compiler_params=_params("parallel"),
        name="mem_kv",
    )(x, g, w)


def _mix_out_kernel(conv_ref, attn_ref, h_ref, w_out_ref, mix_g_ref,
                    xpre_g_ref, wq_ref, kv_ref, wo_ref, xpost_g_ref, o_ref, *, sub):
    cw = conv_ref.shape[1]
    width = wq_ref.shape[1]
    scale = HEAD_DIM ** -0.5
    n_sub = h_ref.shape[0] // sub
    tile = lambda r: slice(r * sub, (r + 1) * sub)

    def mix(r):
        return _dot(conv_ref[tile(r), :], w_out_ref[0:cw, :]) + _dot(attn_ref[tile(r), :], w_out_ref[cw:, :])

    def query(r, y):
        h = h_ref[tile(r), :] + _rms(y, mix_g_ref[...])
        return h, _dot(_rms(h, xpre_g_ref[...]).astype(BF16), wq_ref[...])

    def attend(r, hq):
        h, q = hq
        heads = []
        for hd in range(N_XATTN_HEADS):
            cols = slice(hd * HEAD_DIM, (hd + 1) * HEAD_DIM)
            vcols = slice(width + hd * HEAD_DIM, width + (hd + 1) * HEAD_DIM)
            s = lax.dot_general(q[:, cols].astype(BF16), kv_ref[:, cols], _NT,
                                preferred_element_type=F32) * scale
            e = jnp.exp(s - jnp.max(s, axis=-1, keepdims=True))
            p = e / jnp.sum(e, axis=-1, keepdims=True)
            heads.append(_dot(p.astype(BF16), kv_ref[:, vcols]))
        return h, jnp.concatenate(heads, axis=1).astype(BF16)

    def project(r, ho):
        h, o = ho
        o_ref[tile(r), :] = h + _rms(_dot(o, wo_ref[...]), xpost_g_ref[...])

    stages = [lambda r, _: mix(r), query, attend, project]
    carried = {}
    for step in range(n_sub + len(stages) - 1):
        for k, stage in enumerate(stages):
            r = step - k
            if 0 <= r < n_sub:
                carried[r] = stage(r, carried.get(r))


def _mix_out(conv, attn, h, w_out, mix_g, xpre_g, w_q, kv, w_o, xpost_g, *, tm, seq, sub):
    m, d = h.shape
    mem_len = kv.shape[0] // (m // seq)
    row = lambda i: (i, 0)
    return pl.pallas_call(
        functools.partial(_mix_out_kernel, sub=sub),
        grid=(m // tm,),
        in_specs=[
            pl.BlockSpec((tm, conv.shape[1]), row),
            pl.BlockSpec((tm, attn.shape[1]), row),
            pl.BlockSpec((tm, d), row),
            _resident(w_out.shape),
            _resident((1, d)),
            _resident((1, d)),
            _resident(w_q.shape),
            pl.BlockSpec((mem_len, kv.shape[1]), lambda i: (i // (seq // tm), 0)),
            _resident(w_o.shape),
            _resident((1, d)),
        ],
        out_specs=pl.BlockSpec((tm, d), row),
        out_shape=jax.ShapeDtypeStruct((m, d), F32),
        compiler_params=_params("parallel"),
        name="mix_out",
    )(conv, attn, h, w_out, mix_g, xpre_g, w_q, kv, w_o, xpost_g)


def _rope_tables(seq):
    inv_freq = 1.0 / (ROPE_THETA ** (jnp.arange(0, HEAD_DIM, 2, dtype=F32) / HEAD_DIM))
    ang = jnp.arange(seq, dtype=F32)[:, None] * inv_freq[None, :]
    cos, sin = jnp.cos(ang), jnp.sin(ang)
    return jnp.concatenate([cos, cos], axis=-1), jnp.concatenate([-sin, sin], axis=-1)


def kernel(x, mem, ffn1_pre_g, ffn1_w_gu, ffn1_w_down, ffn1_post_g, mix_pre_g, w_in, conv_w_dw, conv_b_dw, conv_ln_g, conv_ln_b, w_out, mix_post_g, xattn_pre_g, mem_g, xattn_w_q, xattn_w_kv, xattn_w_o, xattn_post_g, ffn2_pre_g, ffn2_w_gu, ffn2_w_down, ffn2_post_g):
    batch, seq, d = x.shape
    mem_len = mem.shape[1]
    cos, sin = _rope_tables(seq)
    row = lambda a, l: a[l][None, :]
    h = x.reshape(batch * seq, d)
    mem2d = mem.reshape(batch * mem_len, d)

    for l in range(ffn1_w_gu.shape[0]):
        h = _ffn(h, row(ffn1_pre_g, l), ffn1_w_gu[l].astype(BF16), ffn1_w_down[l].astype(BF16),
                 row(ffn1_post_g, l), tm=FFN_ROWS, tf=FFN_COLS, sub=FFN_SUB_ROWS)

        y, q, k, v = _in_proj(h, row(mix_pre_g, l), w_in[l].astype(BF16), cos, sin,
                              tm=IN_PROJ_ROWS, seq=seq, sub=IN_PROJ_SUB_ROWS)
        conv = _conv_group(y, conv_w_dw[l], row(conv_b_dw, l), row(conv_ln_g, l), row(conv_ln_b, l),
                           ts=CONV_ROWS, seq=seq, rows=CONV_CHUNK_ROWS)
        attn = _moba(q, k, v, seq=seq, heads_per_step=MOBA_HEADS_PER_STEP)

        kv = _norm_matmul(mem2d, row(mem_g, l), xattn_w_kv[l].astype(BF16), tm=MEM_KV_ROWS,
                          out_dtype=BF16)
        h = _mix_out(conv, attn, h, w_out[l].astype(BF16), row(mix_post_g, l), row(xattn_pre_g, l),
                     xattn_w_q[l].astype(BF16), kv, xattn_w_o[l].astype(BF16), row(xattn_post_g, l),
                     tm=MIX_OUT_ROWS, seq=seq, sub=MIX_OUT_SUB_ROWS)

        h = _ffn(h, row(ffn2_pre_g, l), ffn2_w_gu[l].astype(BF16), ffn2_w_down[l].astype(BF16),
                 row(ffn2_post_g, l), tm=FFN_ROWS, tf=FFN_COLS, sub=FFN_SUB_ROWS)
    return h.reshape(batch, seq, d)
```

```python
import functools
import math

import jax
import jax.numpy as jnp
from jax import lax
from jax.experimental import pallas as pl
from jax.experimental.pallas import tpu as pltpu

F32 = jnp.float32
BF16 = jnp.bfloat16

HEAD_DIM = 128
CONV_KERNEL = 31
MOBA_BLOCK = 256
MOBA_TOPK = 3
N_XATTN_HEADS = 4
ROPE_THETA = 10000.0
RMS_EPS = 1e-6
LN_EPS = 1e-5
FFN_RES_SCALE = 0.5
NEG_INF = -1e30

V7X_VMEM_BYTES = 64 * 1024 * 1024
VMEM_LIMIT_BYTES = V7X_VMEM_BYTES - 3 * 1024 * 1024
SUBLANES = 8
BF16_SUBLANES = 16
CONV_PAD = 32

_NT = (((1,), (1,)), ((), ()))

FFN_ROWS, FFN_SUB_ROWS, FFN_COLS = 1024, 512, 512
IN_PROJ_ROWS, IN_PROJ_SUB_ROWS = 512, 256
CONV_ROWS, CONV_CHUNK_ROWS = 512, 32
MIX_OUT_ROWS, MIX_OUT_SUB_ROWS = 512, 256
MEM_KV_ROWS = 256
MOBA_HEADS_PER_STEP = 4


def _params(*semantics):
    return pltpu.CompilerParams(dimension_semantics=semantics, vmem_limit_bytes=VMEM_LIMIT_BYTES)


def _resident(shape):
    return pl.BlockSpec(shape, lambda *_: (0,) * len(shape), pipeline_mode=pl.Buffered(1))


def _rms(x, g):
    return x * lax.rsqrt(jnp.mean(x * x, axis=-1, keepdims=True) + RMS_EPS) * g


def _dot(a, b):
    return jnp.dot(a, b, preferred_element_type=F32)


def _ffn_kernel(x_ref, pre_g_ref, wg_ref, wu_ref, wd_ref, post_g_ref, o_ref, xn_ref, *, sub):
    j = pl.program_id(1)
    last_j = pl.num_programs(1) - 1
    tiles = [slice(r * sub, (r + 1) * sub) for r in range(o_ref.shape[0] // sub)]
    tf = wd_ref.shape[0]
    chunks = [slice(n * tf, (n + 1) * tf) for n in range(o_ref.shape[1] // tf)]

    def body(first, last):
        for rows in tiles:
            if first:
                xn = _rms(x_ref[rows, :], pre_g_ref[...]).astype(BF16)
                xn_ref[rows, :] = xn
            else:
                xn = xn_ref[rows, :]
            gate = _dot(xn, wg_ref[...])
            up = _dot(xn, wu_ref[...])
            act = (gate * jax.nn.sigmoid(gate) * up).astype(BF16)
            for cols in chunks:
                part = _dot(act, wd_ref[:, cols])
                if first:
                    o_ref[rows, cols] = part
                else:
                    o_ref[rows, cols] += part
            if last:
                o_ref[rows, :] = x_ref[rows, :] + FFN_RES_SCALE * _rms(o_ref[rows, :], post_g_ref[...])

    pl.when(j == 0)(lambda: body(True, False))
    pl.when((j > 0) & (j < last_j))(lambda: body(False, False))
    pl.when(j == last_j)(lambda: body(False, True))


def _ffn(h, pre_g, w_gu, w_down, post_g, *, tm, tf, sub):
    m, d = h.shape
    d_ff = w_down.shape[0]
    nf = d_ff // tf
    assert m % tm == 0 and tm % sub == 0 and d_ff % tf == 0 and d % tf == 0
    assert nf >= 2, "the first and the last d_ff step must be different steps"
    return pl.pallas_call(
        functools.partial(_ffn_kernel, sub=sub),
        grid=(m // tm, nf),
        in_specs=[
            pl.BlockSpec((tm, d), lambda i, j: (i, 0)),
            pl.BlockSpec((1, d), lambda i, j: (0, 0)),
            pl.BlockSpec((d, tf), lambda i, j: (0, j)),
            pl.BlockSpec((d, tf), lambda i, j: (0, j + nf)),
            pl.BlockSpec((tf, d), lambda i, j: (j, 0)),
            pl.BlockSpec((1, d), lambda i, j: (0, 0)),
        ],
        out_specs=pl.BlockSpec((tm, d), lambda i, j: (i, 0)),
        out_shape=jax.ShapeDtypeStruct((m, d), F32),
        scratch_shapes=[pltpu.VMEM((tm, d), BF16)],
        compiler_params=_params("parallel", "arbitrary"),
        name="ffn",
    )(h, pre_g, w_gu, w_gu, w_down, post_g)


def _in_proj_kernel(x_ref, g_ref, w_ref, cos_ref, sin_ref, y_ref, q_ref, k_ref, v_ref, *, sub):
    width = y_ref.shape[1]
    n_heads = width // HEAD_DIM
    col = lambda n: w_ref[:, n * width:(n + 1) * width]

    for r in range(x_ref.shape[0] // sub):
        rows = slice(r * sub, (r + 1) * sub)
        xn = _rms(x_ref[rows, :], g_ref[...]).astype(BF16)
        cos, sin = cos_ref[rows, :], sin_ref[rows, :]

        def rope(z, o_ref):
            for h in range(n_heads):
                cols = slice(h * HEAD_DIM, (h + 1) * HEAD_DIM)
                zh = z[:, cols]
                o_ref[rows, cols] = zh * cos + pltpu.roll(zh, HEAD_DIM // 2, axis=1) * sin

        y_ref[rows, :] = _dot(xn, col(0)) * jax.nn.sigmoid(_dot(xn, col(1)))
        rope(_dot(xn, col(2)), q_ref)
        rope(_dot(xn, col(3)), k_ref)
        v_ref[rows, :] = _dot(xn, col(4)).astype(BF16)


def _in_proj(h, g, w_in, cos, sin, *, tm, seq, sub):
    m, d = h.shape
    width = w_in.shape[1] // 5
    assert seq % tm == 0 and tm % sub == 0 and width % HEAD_DIM == 0
    row = lambda i: (i, 0)
    table = pl.BlockSpec((tm, HEAD_DIM), lambda i: (i % (seq // tm), 0))
    return pl.pallas_call(
        functools.partial(_in_proj_kernel, sub=sub),
        grid=(m // tm,),
        in_specs=[pl.BlockSpec((tm, d), row), _resident((1, d)), _resident(w_in.shape), table, table],
        out_specs=[pl.BlockSpec((tm, width), row)] * 4,
        out_shape=[
            jax.ShapeDtypeStruct((m, width), F32),
            jax.ShapeDtypeStruct((m, width), F32),
            jax.ShapeDtypeStruct((m, width), F32),
            jax.ShapeDtypeStruct((m, width), BF16),
        ],
        compiler_params=_params("parallel"),
        name="in_proj",
    )(h, g, w_in, cos, sin)


def _conv_kernel(y_ref, w_ref, b_ref, ln_g_ref, ln_b_ref, o_ref, win_ref, shift_ref, w8_ref, conv_ref,
                 *, rows):
    t = pl.program_id(1)
    ts, c = y_ref.shape

    @pl.when(t == 0)
    def _():
        win_ref[0:CONV_PAD, :] = jnp.zeros((CONV_PAD, c), F32)
        for tap in range(CONV_KERNEL):
            w8_ref[tap] = jnp.broadcast_to(w_ref[tap:tap + 1, :], (SUBLANES, c))

    @pl.when(t > 0)
    def _():
        win_ref[0:CONV_PAD, :] = win_ref[ts:ts + CONV_PAD, :]

    win_ref[CONV_PAD:CONV_PAD + ts, :] = y_ref[...]
    span = shift_ref.shape[1]
    for b in range(1, SUBLANES):
        shift_ref[b - 1] = win_ref[b:b + span, :]

    first = CONV_PAD - (CONV_KERNEL - 1)
    n_sub = rows // SUBLANES

    def chunk(ci, carry):
        r0 = pl.multiple_of(ci * rows, rows)
        accs = [jnp.broadcast_to(b_ref[...], (SUBLANES, c))] * n_sub
        for tap in range(CONV_KERNEL):
            off, b = divmod(first + tap, SUBLANES)
            src = win_ref if b == 0 else shift_ref.at[b - 1]
            wt = w8_ref[tap]
            for i in range(n_sub):
                start = pl.multiple_of(r0 + (off + i) * SUBLANES, SUBLANES)
                accs[i] = accs[i] + src[pl.ds(start, SUBLANES), :] * wt
        for i in range(n_sub):
            conv_ref[pl.ds(pl.multiple_of(r0 + i * SUBLANES, SUBLANES), SUBLANES), :] = accs[i]
        return carry

    lax.fori_loop(0, ts // rows, chunk, 0, unroll=2)

    acc = conv_ref[...]
    mu = jnp.mean(acc, axis=-1, keepdims=True)
    xc = acc - mu
    var = jnp.mean(xc * xc, axis=-1, keepdims=True)
    yn = xc * lax.rsqrt(var + LN_EPS) * ln_g_ref[...] + ln_b_ref[...]
    o_ref[...] = (yn * jax.nn.sigmoid(yn)).astype(o_ref.dtype)


def _conv_group(y, w_dw, b_dw, ln_g, ln_b, *, ts, seq, rows):
    m, c = y.shape
    nt = seq // ts
    assert seq % ts == 0 and ts % (2 * rows) == 0 and rows % SUBLANES == 0 and CONV_PAD >= CONV_KERNEL - 1
    const = lambda b, t: (0, 0)
    return pl.pallas_call(
        functools.partial(_conv_kernel, rows=rows),
        grid=(m // seq, nt),
        in_specs=[
            pl.BlockSpec((ts, c), lambda b, t: (b * nt + t, 0)),
            pl.BlockSpec((CONV_KERNEL, c), const),
            pl.BlockSpec((1, c), const),
            pl.BlockSpec((1, c), const),
            pl.BlockSpec((1, c), const),
        ],
        out_specs=pl.BlockSpec((ts, c), lambda b, t: (b * nt + t, 0)),
        out_shape=jax.ShapeDtypeStruct((m, c), BF16),
        scratch_shapes=[
            pltpu.VMEM((CONV_PAD + ts, c), F32),
            pltpu.VMEM((SUBLANES - 1, CONV_PAD + ts - SUBLANES, c), F32),
            pltpu.VMEM((CONV_KERNEL, SUBLANES, c), F32),
            pltpu.VMEM((ts, c), F32),
        ],
        compiler_params=_params("parallel", "arbitrary"),
        name="conv_group",
    )(y, w_dw, b_dw, ln_g, ln_b)


def _moba_kernel(q_ref, k_ref, v_ref, o_ref, kb_ref, vt_ref, kmean_ref, p_ref):
    bs = MOBA_BLOCK
    nb = q_ref.shape[0] // bs
    n_heads = q_ref.shape[1] // HEAD_DIM
    c = HEAD_DIM ** -0.5 * math.log2(math.e)
    head = lambda h: slice(h * HEAD_DIM, (h + 1) * HEAD_DIM)

    for h in range(n_heads):
        for j in range(nb):
            blk_rows = slice(j * bs, (j + 1) * bs)
            kj = k_ref[blk_rows, head(h)]
            kb_ref[h, blk_rows, :] = kj.astype(BF16)
            kmean_ref[h, j:j + 1, :] = jnp.mean(kj, axis=0, keepdims=True)
            vt_ref[h, 0:HEAD_DIM, blk_rows] = v_ref[blk_rows, head(h)].astype(F32).T.astype(BF16)
        vt_ref[h, HEAD_DIM:, :] = jnp.ones((vt_ref.shape[1] - HEAD_DIM, vt_ref.shape[2]), BF16)

    blk = lax.broadcasted_iota(jnp.int32, (nb, bs), 0)
    causal = (lax.broadcasted_iota(jnp.int32, (bs, bs), 0)
              <= lax.broadcasted_iota(jnp.int32, (bs, bs), 1))

    def scores(h, qi):
        qt = q_ref[qi * bs:(qi + 1) * bs, head(h)].T
        s = _dot(kb_ref[h, 0:(qi + 1) * bs, :], (qt * c).astype(BF16))
        keep = None
        if qi > MOBA_TOPK:
            gate = jnp.dot(kmean_ref[h], qt, precision=lax.Precision.HIGHEST, preferred_element_type=F32)
            beaten_by = jnp.zeros((nb, bs), jnp.int32)
            for i in range(qi):
                gi = gate[i:i + 1, :]
                beats = (gi > gate) | ((gi == gate) & (i < blk))
                beaten_by = beaten_by + beats.astype(jnp.int32)
            keep = jnp.where((blk < qi) & (beaten_by < MOBA_TOPK), 1.0, 0.0)
        return s, keep

    def softmax(slot, qi, s, keep):
        blocks = []
        for j in range(qi + 1):
            sj = s[j * bs:(j + 1) * bs, :]
            if j == qi:
                sj = jnp.where(causal, sj, NEG_INF)
            elif keep is not None:
                sj = jnp.where(keep[j:j + 1, :] > 0.5, sj, NEG_INF)
            blocks.append(sj)
        m = functools.reduce(jnp.maximum, [jnp.max(sj, axis=0, keepdims=True) for sj in blocks])
        for j, sj in enumerate(blocks):
            p_ref[slot, j * bs:(j + 1) * bs, :] = jnp.exp2(sj - m).astype(BF16)

    def weighted_values(slot, h, qi):
        nk = (qi + 1) * bs
        acc = _dot(vt_ref[h, :, 0:nk], p_ref[slot, 0:nk, :])
        out = acc[0:HEAD_DIM, :] / acc[HEAD_DIM:HEAD_DIM + 1, :]
        o_ref[qi * bs:(qi + 1) * bs, head(h)] = out.T.astype(o_ref.dtype)

    items = [(h, qi) for qi in range(nb - 1, -1, -1) for h in range(n_heads)]
    scored = {}
    for step in range(len(items) + 2):
        if step < len(items):
            scored[step] = scores(*items[step])
        if 0 <= step - 1 < len(items):
            softmax((step - 1) % 2, items[step - 1][1], *scored.pop(step - 1))
        if 0 <= step - 2 < len(items):
            weighted_values((step - 2) % 2, *items[step - 2])


def _moba(q, k, v, *, seq, heads_per_step):
    m, width = q.shape
    cols = heads_per_step * HEAD_DIM
    assert seq % MOBA_BLOCK == 0 and width % cols == 0
    spec = pl.BlockSpec((seq, cols), lambda b, h: (b, h))
    return pl.pallas_call(
        _moba_kernel,
        grid=(m // seq, width // cols),
        in_specs=[spec, spec, spec],
        out_specs=spec,
        out_shape=jax.ShapeDtypeStruct((m, width), BF16),
        scratch_shapes=[
            pltpu.VMEM((heads_per_step, seq, HEAD_DIM), BF16),
            pltpu.VMEM((heads_per_step, HEAD_DIM + BF16_SUBLANES, seq), BF16),
            pltpu.VMEM((heads_per_step, seq // MOBA_BLOCK, HEAD_DIM), F32),
            pltpu.VMEM((2, seq, MOBA_BLOCK), BF16),
        ],
        compiler_params=_params("parallel", "parallel"),
        name="moba",
    )(q, k, v)


def _norm_matmul_kernel(x_ref, g_ref, w_ref, o_ref):
    xn = _rms(x_ref[...], g_ref[...]).astype(BF16)
    o_ref[...] = _dot(xn, w_ref[...]).astype(o_ref.dtype)


def _norm_matmul(x, g, w, *, tm, out_dtype):
    m, d = x.shape
    n = w.shape[1]
    return pl.pallas_call(
        _norm_matmul_kernel,
        grid=(m // tm,),
        in_specs=[pl.BlockSpec((tm, d), lambda i: (i, 0)), _resident((1, d)), _resident((d, n))],
        out_specs=pl.BlockSpec((tm, n), lambda i: (i, 0)),
        out_shape=jax.ShapeDtypeStruct((m, n), out_dtype),
        compiler_params=_params("parallel"),
        name="mem_kv",
    )(x, g, w)


def _mix_out_kernel(conv_ref, attn_ref, h_ref, w_out_ref, mix_g_ref,
                    xpre_g_ref, wq_ref, kv_ref, wo_ref, xpost_g_ref, o_ref, *, sub):
    cw = conv_ref.shape[1]
    width = wq_ref.shape[1]
    scale = HEAD_DIM ** -0.5
    n_sub = h_ref.shape[0] // sub
    tile = lambda r: slice(r * sub, (r + 1) * sub)

    def mix(r):
        return _dot(conv_ref[tile(r), :], w_out_ref[0:cw, :]) + _dot(attn_ref[tile(r), :], w_out_ref[cw:, :])

    def query(r, y):
        h = h_ref[tile(r), :] + _rms(y, mix_g_ref[...])
        return h, _dot(_rms(h, xpre_g_ref[...]).astype(BF16), wq_ref[...])

    def attend(r, hq):
        h, q = hq
        heads = []
        for hd in range(N_XATTN_HEADS):
            cols = slice(hd * HEAD_DIM, (hd + 1) * HEAD_DIM)
            vcols = slice(width + hd * HEAD_DIM, width + (hd + 1) * HEAD_DIM)
            s = lax.dot_general(q[:, cols].astype(BF16), kv_ref[:, cols], _NT,
                                preferred_element_type=F32) * scale
            e = jnp.exp(s - jnp.max(s, axis=-1, keepdims=True))
            p = e / jnp.sum(e, axis=-1, keepdims=True)
            heads.append(_dot(p.astype(BF16), kv_ref[:, vcols]))
        return h, jnp.concatenate(heads, axis=1).astype(BF16)

    def project(r, ho):
        h, o = ho
        o_ref[tile(r), :] = h + _rms(_dot(o, wo_ref[...]), xpost_g_ref[...])

    stages = [lambda r, _: mix(r), query, attend, project]
    carried = {}
    for step in range(n_sub + len(stages) - 1):
        for k, stage in enumerate(stages):
            r = step - k
            if 0 <= r < n_sub:
                carried[r] = stage(r, carried.get(r))


def _mix_out(conv, attn, h, w_out, mix_g, xpre_g, w_q, kv, w_o, xpost_g, *, tm, seq, sub):
    m, d = h.shape
    mem_len = kv.shape[0] // (m // seq)
    assert seq % tm == 0 and tm % sub == 0 and w_out.shape[0] == conv.shape[1] + attn.shape[1]
    row = lambda i: (i, 0)
    return pl.pallas_call(
        functools.partial(_mix_out_kernel, sub=sub),
        grid=(m // tm,),
        in_specs=[
            pl.BlockSpec((tm, conv.shape[1]), row),
            pl.BlockSpec((tm, attn.shape[1]), row),
            pl.BlockSpec((tm, d), row),
            _resident(w_out.shape),
            _resident((1, d)),
            _resident((1, d)),
            _resident(w_q.shape),
            pl.BlockSpec((mem_len, kv.shape[1]), lambda i: (i // (seq // tm), 0)),
            _resident(w_o.shape),
            _resident((1, d)),
        ],
        out_specs=pl.BlockSpec((tm, d), row),
        out_shape=jax.ShapeDtypeStruct((m, d), F32),
        compiler_params=_params("parallel"),
        name="mix_out",
    )(conv, attn, h, w_out, mix_g, xpre_g, w_q, kv, w_o, xpost_g)


def _rope_tables(seq):
    inv_freq = 1.0 / (ROPE_THETA ** (jnp.arange(0, HEAD_DIM, 2, dtype=F32) / HEAD_DIM))
    ang = jnp.arange(seq, dtype=F32)[:, None] * inv_freq[None, :]
    cos, sin = jnp.cos(ang), jnp.sin(ang)
    return jnp.concatenate([cos, cos], axis=-1), jnp.concatenate([-sin, sin], axis=-1)


def kernel(x, mem, ffn1_pre_g, ffn1_w_gu, ffn1_w_down, ffn1_post_g, mix_pre_g, w_in, conv_w_dw, conv_b_dw, conv_ln_g, conv_ln_b, w_out, mix_post_g, xattn_pre_g, mem_g, xattn_w_q, xattn_w_kv, xattn_w_o, xattn_post_g, ffn2_pre_g, ffn2_w_gu, ffn2_w_down, ffn2_post_g):
    batch, seq, d = x.shape
    mem_len = mem.shape[1]
    cos, sin = _rope_tables(seq)
    row = lambda a, l: a[l][None, :]
    h = x.reshape(batch * seq, d)
    mem2d = mem.reshape(batch * mem_len, d)

    for l in range(ffn1_w_gu.shape[0]):
        h = _ffn(h, row(ffn1_pre_g, l), ffn1_w_gu[l].astype(BF16), ffn1_w_down[l].astype(BF16),
                 row(ffn1_post_g, l), tm=FFN_ROWS, tf=FFN_COLS, sub=FFN_SUB_ROWS)

        y, q, k, v = _in_proj(h, row(mix_pre_g, l), w_in[l].astype(BF16), cos, sin,
                              tm=IN_PROJ_ROWS, seq=seq, sub=IN_PROJ_SUB_ROWS)
        conv = _conv_group(y, conv_w_dw[l], row(conv_b_dw, l), row(conv_ln_g, l), row(conv_ln_b, l),
                           ts=CONV_ROWS, seq=seq, rows=CONV_CHUNK_ROWS)
        attn = _moba(q, k, v, seq=seq, heads_per_step=MOBA_HEADS_PER_STEP)

        kv = _norm_matmul(mem2d, row(mem_g, l), xattn_w_kv[l].astype(BF16), tm=MEM_KV_ROWS,
                          out_dtype=BF16)
        h = _mix_out(conv, attn, h, w_out[l].astype(BF16), row(mix_post_g, l), row(xattn_pre_g, l),
                     xattn_w_q[l].astype(BF16), kv, xattn_w_o[l].astype(BF16), row(xattn_post_g, l),
                     tm=MIX_OUT_ROWS, seq=seq, sub=MIX_OUT_SUB_ROWS)

        h = _ffn(h, row(ffn2_pre_g, l), ffn2_w_gu[l].astype(BF16), ffn2_w_down[l].astype(BF16),
                 row(ffn2_post_g, l), tm=FFN_ROWS, tf=FFN_COLS, sub=FFN_SUB_ROWS)
    return h.reshape(batch, seq, d)
```

```python
import functools
import math

import jax
import jax.numpy as jnp
from jax import lax
from jax.experimental import pallas as pl
from jax.experimental.pallas import tpu as pltpu

F32 = jnp.float32
BF16 = jnp.bfloat16

HEAD_DIM = 128
CONV_KERNEL = 31
MOBA_BLOCK = 256
MOBA_TOPK = 3
N_XATTN_HEADS = 4
ROPE_THETA = 10000.0
RMS_EPS = 1e-6
LN_EPS = 1e-5
FFN_RES_SCALE = 0.5
NEG_INF = -1e30

V7X_VMEM_BYTES = 64 * 1024 * 1024
VMEM_LIMIT_BYTES = V7X_VMEM_BYTES - 3 * 1024 * 1024
SUBLANES = 8
BF16_SUBLANES = 16
CONV_PAD = 32

_NT = (((1,), (1,)), ((), ()))

FFN_ROWS, FFN_SUB_ROWS, FFN_COLS = 1024, 512, 512
IN_PROJ_ROWS, IN_PROJ_SUB_ROWS = 512, 256
CONV_ROWS, CONV_CHUNK_ROWS = 512, 32
MIX_OUT_ROWS, MIX_OUT_SUB_ROWS = 512, 256
MEM_KV_ROWS = 256
MOBA_HEADS_PER_STEP = 2


def _params(*semantics):
    return pltpu.CompilerParams(dimension_semantics=semantics, vmem_limit_bytes=VMEM_LIMIT_BYTES)


def _resident(shape):
    return pl.BlockSpec(shape, lambda *_: (0,) * len(shape), pipeline_mode=pl.Buffered(1))


def _rms(x, g):
    return x * lax.rsqrt(jnp.mean(x * x, axis=-1, keepdims=True) + RMS_EPS) * g


def _dot(a, b):
    return jnp.dot(a, b, preferred_element_type=F32)


def _ffn_kernel(x_ref, pre_g_ref, wg_ref, wu_ref, wd_ref, post_g_ref, o_ref, xn_ref, *, sub):
    j = pl.program_id(1)
    last_j = pl.num_programs(1) - 1
    tiles = [slice(r * sub, (r + 1) * sub) for r in range(o_ref.shape[0] // sub)]
    tf = wd_ref.shape[0]
    chunks = [slice(n * tf, (n + 1) * tf) for n in range(o_ref.shape[1] // tf)]

    def body(first, last):
        for rows in tiles:
            if first:
                xn = _rms(x_ref[rows, :], pre_g_ref[...]).astype(BF16)
                xn_ref[rows, :] = xn
            else:
                xn = xn_ref[rows, :]
            gate = _dot(xn, wg_ref[...])
            up = _dot(xn, wu_ref[...])
            act = (gate * jax.nn.sigmoid(gate) * up).astype(BF16)
            for cols in chunks:
                part = _dot(act, wd_ref[:, cols])
                if first:
                    o_ref[rows, cols] = part
                else:
                    o_ref[rows, cols] += part
            if last:
                o_ref[rows, :] = x_ref[rows, :] + FFN_RES_SCALE * _rms(o_ref[rows, :], post_g_ref[...])

    pl.when(j == 0)(lambda: body(True, False))
    pl.when((j > 0) & (j < last_j))(lambda: body(False, False))
    pl.when(j == last_j)(lambda: body(False, True))


def _ffn(h, pre_g, w_gu, w_down, post_g, *, tm, tf, sub):
    m, d = h.shape
    d_ff = w_down.shape[0]
    nf = d_ff // tf
    assert m % tm == 0 and tm % sub == 0 and d_ff % tf == 0 and d % tf == 0
    assert nf >= 2, "the first and the last d_ff step must be different steps"
    return pl.pallas_call(
        functools.partial(_ffn_kernel, sub=sub),
        grid=(m // tm, nf),
        in_specs=[
            pl.BlockSpec((tm, d), lambda i, j: (i, 0)),
            pl.BlockSpec((1, d), lambda i, j: (0, 0)),
            pl.BlockSpec((d, tf), lambda i, j: (0, j)),
            pl.BlockSpec((d, tf), lambda i, j: (0, j + nf)),
            pl.BlockSpec((tf, d), lambda i, j: (j, 0)),
            pl.BlockSpec((1, d), lambda i, j: (0, 0)),
        ],
        out_specs=pl.BlockSpec((tm, d), lambda i, j: (i, 0)),
        out_shape=jax.ShapeDtypeStruct((m, d), F32),
        scratch_shapes=[pltpu.VMEM((tm, d), BF16)],
        compiler_params=_params("parallel", "arbitrary"),
        name="ffn",
    )(h, pre_g, w_gu, w_gu, w_down, post_g)


def _in_proj_kernel(x_ref, g_ref, w_ref, cos_ref, sin_ref, y_ref, q_ref, k_ref, v_ref, *, sub):
    width = y_ref.shape[1]
    n_heads = width // HEAD_DIM
    col = lambda n: w_ref[:, n * width:(n + 1) * width]

    for r in range(x_ref.shape[0] // sub):
        rows = slice(r * sub, (r + 1) * sub)
        xn = _rms(x_ref[rows, :], g_ref[...]).astype(BF16)
        cos, sin = cos_ref[rows, :], sin_ref[rows, :]

        def rope(z, o_ref):
            for h in range(n_heads):
                cols = slice(h * HEAD_DIM, (h + 1) * HEAD_DIM)
                zh = z[:, cols]
                o_ref[rows, cols] = zh * cos + pltpu.roll(zh, HEAD_DIM // 2, axis=1) * sin

        y_ref[rows, :] = _dot(xn, col(0)) * jax.nn.sigmoid(_dot(xn, col(1)))
        rope(_dot(xn, col(2)), q_ref)
        rope(_dot(xn, col(3)), k_ref)
        v_ref[rows, :] = _dot(xn, col(4)).astype(BF16)


def _in_proj(h, g, w_in, cos, sin, *, tm, seq, sub):
    m, d = h.shape
    width = w_in.shape[1] // 5
    assert seq % tm == 0 and tm % sub == 0 and width % HEAD_DIM == 0
    row = lambda i: (i, 0)
    table = pl.BlockSpec((tm, HEAD_DIM), lambda i: (i % (seq // tm), 0))
    return pl.pallas_call(
        functools.partial(_in_proj_kernel, sub=sub),
        grid=(m // tm,),
        in_specs=[pl.BlockSpec((tm, d), row), _resident((1, d)), _resident(w_in.shape), table, table],
        out_specs=[pl.BlockSpec((tm, width), row)] * 4,
        out_shape=[
            jax.ShapeDtypeStruct((m, width), F32),
            jax.ShapeDtypeStruct((m, width), F32),
            jax.ShapeDtypeStruct((m, width), F32),
            jax.ShapeDtypeStruct((m, width), BF16),
        ],
        compiler_params=_params("parallel"),
        name="in_proj",
    )(h, g, w_in, cos, sin)


def _conv_kernel(y_ref, w_ref, b_ref, ln_g_ref, ln_b_ref, o_ref, win_ref, shift_ref, w8_ref, conv_ref,
                 *, rows):
    t = pl.program_id(1)
    ts, c = y_ref.shape

    @pl.when(t == 0)
    def _():
        win_ref[0:CONV_PAD, :] = jnp.zeros((CONV_PAD, c), F32)
        for tap in range(CONV_KERNEL):
            w8_ref[tap] = jnp.broadcast_to(w_ref[tap:tap + 1, :], (SUBLANES, c))

    @pl.when(t > 0)
    def _():
        win_ref[0:CONV_PAD, :] = win_ref[ts:ts + CONV_PAD, :]

    win_ref[CONV_PAD:CONV_PAD + ts, :] = y_ref[...]
    span = shift_ref.shape[1]
    for b in range(1, SUBLANES):
        shift_ref[b - 1] = win_ref[b:b + span, :]

    first = CONV_PAD - (CONV_KERNEL - 1)
    n_sub = rows // SUBLANES

    def chunk(ci, carry):
        r0 = pl.multiple_of(ci * rows, rows)
        accs = [jnp.broadcast_to(b_ref[...], (SUBLANES, c))] * n_sub
        for tap in range(CONV_KERNEL):
            off, b = divmod(first + tap, SUBLANES)
            src = win_ref if b == 0 else shift_ref.at[b - 1]
            wt = w8_ref[tap]
            for i in range(n_sub):
                start = pl.multiple_of(r0 + (off + i) * SUBLANES, SUBLANES)
                accs[i] = accs[i] + src[pl.ds(start, SUBLANES), :] * wt
        for i in range(n_sub):
            conv_ref[pl.ds(pl.multiple_of(r0 + i * SUBLANES, SUBLANES), SUBLANES), :] = accs[i]
        return carry

    lax.fori_loop(0, ts // rows, chunk, 0, unroll=2)

    acc = conv_ref[...]
    mu = jnp.mean(acc, axis=-1, keepdims=True)
    xc = acc - mu
    var = jnp.mean(xc * xc, axis=-1, keepdims=True)
    yn = xc * lax.rsqrt(var + LN_EPS) * ln_g_ref[...] + ln_b_ref[...]
    o_ref[...] = (yn * jax.nn.sigmoid(yn)).astype(o_ref.dtype)


def _conv_group(y, w_dw, b_dw, ln_g, ln_b, *, ts, seq, rows):
    m, c = y.shape
    nt = seq // ts
    assert seq % ts == 0 and ts % (2 * rows) == 0 and rows % SUBLANES == 0 and CONV_PAD >= CONV_KERNEL - 1
    const = lambda b, t: (0, 0)
    return pl.pallas_call(
        functools.partial(_conv_kernel, rows=rows),
        grid=(m // seq, nt),
        in_specs=[
            pl.BlockSpec((ts, c), lambda b, t: (b * nt + t, 0)),
            pl.BlockSpec((CONV_KERNEL, c), const),
            pl.BlockSpec((1, c), const),
            pl.BlockSpec((1, c), const),
            pl.BlockSpec((1, c), const),
        ],
        out_specs=pl.BlockSpec((ts, c), lambda b, t: (b * nt + t, 0)),
        out_shape=jax.ShapeDtypeStruct((m, c), BF16),
        scratch_shapes=[
            pltpu.VMEM((CONV_PAD + ts, c), F32),
            pltpu.VMEM((SUBLANES - 1, CONV_PAD + ts - SUBLANES, c), F32),
            pltpu.VMEM((CONV_KERNEL, SUBLANES, c), F32),
            pltpu.VMEM((ts, c), F32),
        ],
        compiler_params=_params("parallel", "arbitrary"),
        name="conv_group",
    )(y, w_dw, b_dw, ln_g, ln_b)


def _moba_kernel(q_ref, k_ref, v_ref, *refs, n_cast):
    cast_in, o_ref, cast_out = refs[:n_cast], refs[n_cast], refs[n_cast + 1:2 * n_cast + 1]
    kb_ref, vt_ref, kmean_ref, p_ref = refs[2 * n_cast + 1:]
    for src, dst in zip(cast_in, cast_out):
        dst[...] = src[...].astype(dst.dtype)

    bs = MOBA_BLOCK
    nb = q_ref.shape[0] // bs
    n_heads = q_ref.shape[1] // HEAD_DIM
    c = HEAD_DIM ** -0.5 * math.log2(math.e)
    head = lambda h: slice(h * HEAD_DIM, (h + 1) * HEAD_DIM)

    for h in range(n_heads):
        for j in range(nb):
            blk_rows = slice(j * bs, (j + 1) * bs)
            kj = k_ref[blk_rows, head(h)]
            kb_ref[h, blk_rows, :] = kj.astype(BF16)
            kmean_ref[h, j:j + 1, :] = jnp.mean(kj, axis=0, keepdims=True)
            vt_ref[h, 0:HEAD_DIM, blk_rows] = v_ref[blk_rows, head(h)].astype(F32).T.astype(BF16)
        vt_ref[h, HEAD_DIM:, :] = jnp.ones((vt_ref.shape[1] - HEAD_DIM, vt_ref.shape[2]), BF16)

    blk = lax.broadcasted_iota(jnp.int32, (nb, bs), 0)
    causal = (lax.broadcasted_iota(jnp.int32, (bs, bs), 0)
              <= lax.broadcasted_iota(jnp.int32, (bs, bs), 1))

    def scores(h, qi):
        qt = q_ref[qi * bs:(qi + 1) * bs, head(h)].T
        s = _dot(kb_ref[h, 0:(qi + 1) * bs, :], (qt * c).astype(BF16))
        keep = None
        if qi > MOBA_TOPK:
            gate = jnp.dot(kmean_ref[h], qt, precision=lax.Precision.HIGHEST, preferred_element_type=F32)
            beaten_by = jnp.zeros((nb, bs), jnp.int32)
            for i in range(qi):
                gi = gate[i:i + 1, :]
                beats = (gi > gate) | ((gi == gate) & (i < blk))
                beaten_by = beaten_by + beats.astype(jnp.int32)
            keep = jnp.where((blk < qi) & (beaten_by < MOBA_TOPK), 1.0, 0.0)
        return s, keep

    def softmax(slot, qi, s, keep):
        blocks = []
        for j in range(qi + 1):
            sj = s[j * bs:(j + 1) * bs, :]
            if j == qi:
                sj = jnp.where(causal, sj, NEG_INF)
            elif keep is not None:
                sj = jnp.where(keep[j:j + 1, :] > 0.5, sj, NEG_INF)
            blocks.append(sj)
        m = functools.reduce(jnp.maximum, [jnp.max(sj, axis=0, keepdims=True) for sj in blocks])
        for j, sj in enumerate(blocks):
            p_ref[slot, j * bs:(j + 1) * bs, :] = jnp.exp2(sj - m).astype(BF16)

    def weighted_values(slot, h, qi):
        nk = (qi + 1) * bs
        acc = _dot(vt_ref[h, :, 0:nk], p_ref[slot, 0:nk, :])
        out = acc[0:HEAD_DIM, :] / acc[HEAD_DIM:HEAD_DIM + 1, :]
        o_ref[qi * bs:(qi + 1) * bs, head(h)] = out.T.astype(o_ref.dtype)

    items = [(h, qi) for qi in range(nb - 1, -1, -1) for h in range(n_heads)]
    scored = {}
    for step in range(len(items) + 2):
        if step < len(items):
            scored[step] = scores(*items[step])
        if 0 <= step - 1 < len(items):
            softmax((step - 1) % 2, items[step - 1][1], *scored.pop(step - 1))
        if 0 <= step - 2 < len(items):
            weighted_values((step - 2) % 2, *items[step - 2])


def _moba(q, k, v, to_cast, *, seq, heads_per_step):
    m, width = q.shape
    cols = heads_per_step * HEAD_DIM
    assert seq % MOBA_BLOCK == 0 and width % cols == 0
    groups = width // cols
    steps = (m // seq) * groups
    assert all(w.shape[0] % (steps * BF16_SUBLANES) == 0 for w in to_cast)
    spec = pl.BlockSpec((seq, cols), lambda b, h: (b, h))
    slabs = [pl.BlockSpec((w.shape[0] // steps, w.shape[1]), lambda b, h: (b * groups + h, 0)) for w in to_cast]
    out = pl.pallas_call(
        functools.partial(_moba_kernel, n_cast=len(to_cast)),
        grid=(m // seq, groups),
        in_specs=[spec, spec, spec] + slabs,
        out_specs=[spec] + slabs,
        out_shape=[jax.ShapeDtypeStruct((m, width), BF16)] + [jax.ShapeDtypeStruct(w.shape, BF16) for w in to_cast],
        scratch_shapes=[
            pltpu.VMEM((heads_per_step, seq, HEAD_DIM), BF16),
            pltpu.VMEM((heads_per_step, HEAD_DIM + BF16_SUBLANES, seq), BF16),
            pltpu.VMEM((heads_per_step, seq // MOBA_BLOCK, HEAD_DIM), F32),
            pltpu.VMEM((2, seq, MOBA_BLOCK), BF16),
        ],
        compiler_params=_params("parallel", "parallel"),
        name="moba",
    )(q, k, v, *to_cast)
    return out[0], out[1:]


def _norm_matmul_kernel(x_ref, g_ref, w_ref, o_ref):
    xn = _rms(x_ref[...], g_ref[...]).astype(BF16)
    o_ref[...] = _dot(xn, w_ref[...]).astype(o_ref.dtype)


def _norm_matmul(x, g, w, *, tm, out_dtype):
    m, d = x.shape
    n = w.shape[1]
    return pl.pallas_call(
        _norm_matmul_kernel,
        grid=(m // tm,),
        in_specs=[pl.BlockSpec((tm, d), lambda i: (i, 0)), _resident((1, d)), _resident((d, n))],
        out_specs=pl.BlockSpec((tm, n), lambda i: (i, 0)),
        out_shape=jax.ShapeDtypeStruct((m, n), out_dtype),
        compiler_params=_params("parallel"),
        name="mem_kv",
    )(x, g, w)


def _mix_out_kernel(conv_ref, attn_ref, h_ref, w_out_ref, mix_g_ref,
                    xpre_g_ref, wq_ref, kv_ref, wo_ref, xpost_g_ref, o_ref, *, sub):
    cw = conv_ref.shape[1]
    width = wq_ref.shape[1]
    scale = HEAD_DIM ** -0.5
    n_sub = h_ref.shape[0] // sub
    tile = lambda r: slice(r * sub, (r + 1) * sub)

    def mix(r):
        return _dot(conv_ref[tile(r), :], w_out_ref[0:cw, :]) + _dot(attn_ref[tile(r), :], w_out_ref[cw:, :])

    def query(r, y):
        h = h_ref[tile(r), :] + _rms(y, mix_g_ref[...])
        return h, _dot(_rms(h, xpre_g_ref[...]).astype(BF16), wq_ref[...])

    def attend(r, hq):
        h, q = hq
        heads = []
        for hd in range(N_XATTN_HEADS):
            cols = slice(hd * HEAD_DIM, (hd + 1) * HEAD_DIM)
            vcols = slice(width + hd * HEAD_DIM, width + (hd + 1) * HEAD_DIM)
            s = lax.dot_general(q[:, cols].astype(BF16), kv_ref[:, cols], _NT,
                                preferred_element_type=F32) * scale
            e = jnp.exp(s - jnp.max(s, axis=-1, keepdims=True))
            p = e / jnp.sum(e, axis=-1, keepdims=True)
            heads.append(_dot(p.astype(BF16), kv_ref[:, vcols]))
        return h, jnp.concatenate(heads, axis=1).astype(BF16)

    def project(r, ho):
        h, o = ho
        o_ref[tile(r), :] = h + _rms(_dot(o, wo_ref[...]), xpost_g_ref[...])

    stages = [lambda r, _: mix(r), query, attend, project]
    carried = {}
    for step in range(n_sub + len(stages) - 1):
        for k, stage in enumerate(stages):
            r = step - k
            if 0 <= r < n_sub:
                carried[r] = stage(r, carried.get(r))


def _mix_out(conv, attn, h, w_out, mix_g, xpre_g, w_q, kv, w_o, xpost_g, *, tm, seq, sub):
    m, d = h.shape
    mem_len = kv.shape[0] // (m // seq)
    assert seq % tm == 0 and tm % sub == 0 and w_out.shape[0] == conv.shape[1] + attn.shape[1]
    row = lambda i: (i, 0)
    return pl.pallas_call(
        functools.partial(_mix_out_kernel, sub=sub),
        grid=(m // tm,),
        in_specs=[
            pl.BlockSpec((tm, conv.shape[1]), row),
            pl.BlockSpec((tm, attn.shape[1]), row),
            pl.BlockSpec((tm, d), row),
            _resident(w_out.shape),
            _resident((1, d)),
            _resident((1, d)),
            _resident(w_q.shape),
            pl.BlockSpec((mem_len, kv.shape[1]), lambda i: (i // (seq // tm), 0)),
            _resident(w_o.shape),
            _resident((1, d)),
        ],
        out_specs=pl.BlockSpec((tm, d), row),
        out_shape=jax.ShapeDtypeStruct((m, d), F32),
        compiler_params=_params("parallel"),
        name="mix_out",
    )(conv, attn, h, w_out, mix_g, xpre_g, w_q, kv, w_o, xpost_g)


def _rope_tables(seq):
    inv_freq = 1.0 / (ROPE_THETA ** (jnp.arange(0, HEAD_DIM, 2, dtype=F32) / HEAD_DIM))
    ang = jnp.arange(seq, dtype=F32)[:, None] * inv_freq[None, :]
    cos, sin = jnp.cos(ang), jnp.sin(ang)
    return jnp.concatenate([cos, cos], axis=-1), jnp.concatenate([-sin, sin], axis=-1)


def kernel(x, mem, ffn1_pre_g, ffn1_w_gu, ffn1_w_down, ffn1_post_g, mix_pre_g, w_in, conv_w_dw, conv_b_dw, conv_ln_g, conv_ln_b, w_out, mix_post_g, xattn_pre_g, mem_g, xattn_w_q, xattn_w_kv, xattn_w_o, xattn_post_g, ffn2_pre_g, ffn2_w_gu, ffn2_w_down, ffn2_post_g):
    batch, seq, d = x.shape
    mem_len = mem.shape[1]
    cos, sin = _rope_tables(seq)
    row = lambda a, l: a[l][None, :]
    h = x.reshape(batch * seq, d)
    mem2d = mem.reshape(batch * mem_len, d)

    for l in range(ffn1_w_gu.shape[0]):
        h = _ffn(h, row(ffn1_pre_g, l), ffn1_w_gu[l].astype(BF16), ffn1_w_down[l].astype(BF16),
                 row(ffn1_post_g, l), tm=FFN_ROWS, tf=FFN_COLS, sub=FFN_SUB_ROWS)

        y, q, k, v = _in_proj(h, row(mix_pre_g, l), w_in[l].astype(BF16), cos, sin,
                              tm=IN_PROJ_ROWS, seq=seq, sub=IN_PROJ_SUB_ROWS)
        conv = _conv_group(y, conv_w_dw[l], row(conv_b_dw, l), row(conv_ln_g, l), row(conv_ln_b, l),
                           ts=CONV_ROWS, seq=seq, rows=CONV_CHUNK_ROWS)
        attn, (w_out_b, w_q_b, w_o_b, ffn2_w_gu_b, ffn2_w_down_b) = _moba(
            q, k, v, (w_out[l], xattn_w_q[l], xattn_w_o[l], ffn2_w_gu[l], ffn2_w_down[l]),
            seq=seq, heads_per_step=MOBA_HEADS_PER_STEP)

        kv = _norm_matmul(mem2d, row(mem_g, l), xattn_w_kv[l].astype(BF16), tm=MEM_KV_ROWS,
                          out_dtype=BF16)
        h = _mix_out(conv, attn, h, w_out_b, row(mix_post_g, l), row(xattn_pre_g, l),
                     w_q_b, kv, w_o_b, row(xattn_post_g, l),
                     tm=MIX_OUT_ROWS, seq=seq, sub=MIX_OUT_SUB_ROWS)

        h = _ffn(h, row(ffn2_pre_g, l), ffn2_w_gu_b, ffn2_w_down_b,
                 row(ffn2_post_g, l), tm=FFN_ROWS, tf=FFN_COLS, sub=FFN_SUB_ROWS)
    return h.reshape(batch, seq, d)
```

```python
import functools
import math

import jax
import jax.numpy as jnp
from jax import lax
from jax.experimental import pallas as pl
from jax.experimental.pallas import tpu as pltpu

F32 = jnp.float32
BF16 = jnp.bfloat16

HEAD_DIM = 128
CONV_KERNEL = 31
MOBA_BLOCK = 256
MOBA_TOPK = 3
N_XATTN_HEADS = 4
ROPE_THETA = 10000.0
RMS_EPS = 1e-6
LN_EPS = 1e-5
FFN_RES_SCALE = 0.5
NEG_INF = -1e30

V7X_VMEM_BYTES = 64 * 1024 * 1024
VMEM_LIMIT_BYTES = V7X_VMEM_BYTES - 3 * 1024 * 1024
SUBLANES = 8
BF16_SUBLANES = 16
CONV_PAD = 32

_NT = (((1,), (1,)), ((), ()))

FFN_ROWS, FFN_SUB_ROWS, FFN_COLS = 1024, 512, 512
IN_PROJ_ROWS, IN_PROJ_SUB_ROWS = 512, 256
CONV_ROWS, CONV_CHUNK_ROWS = 512, 32
MIX_OUT_ROWS, MIX_OUT_SUB_ROWS = 512, 256
MEM_KV_ROWS = 256
MOBA_HEADS_PER_STEP = 2


def _params(*semantics):
    return pltpu.CompilerParams(dimension_semantics=semantics, vmem_limit_bytes=VMEM_LIMIT_BYTES)


def _resident(shape):
    return pl.BlockSpec(shape, lambda *_: (0,) * len(shape), pipeline_mode=pl.Buffered(1))


def _rms(x, g):
    return x * lax.rsqrt(jnp.mean(x * x, axis=-1, keepdims=True) + RMS_EPS) * g


def _dot(a, b):
    return jnp.dot(a, b, preferred_element_type=F32)


def _ffn_kernel(x_ref, pre_g_ref, wg_ref, wu_ref, wd_ref, post_g_ref, *refs, sub, n_cast):
    cast_in, o_ref, cast_out, xn_ref = refs[:n_cast], refs[n_cast], refs[n_cast + 1:2 * n_cast + 1], refs[-1]
    for src, dst in zip(cast_in, cast_out):
        dst[...] = src[...].astype(dst.dtype)

    j = pl.program_id(1)
    last_j = pl.num_programs(1) - 1
    tiles = [slice(r * sub, (r + 1) * sub) for r in range(o_ref.shape[0] // sub)]
    tf = wd_ref.shape[0]
    chunks = [slice(n * tf, (n + 1) * tf) for n in range(o_ref.shape[1] // tf)]

    def body(first, last):
        for rows in tiles:
            if first:
                xn = _rms(x_ref[rows, :], pre_g_ref[...]).astype(BF16)
                xn_ref[rows, :] = xn
            else:
                xn = xn_ref[rows, :]
            gate = _dot(xn, wg_ref[...])
            up = _dot(xn, wu_ref[...])
            act = (gate * jax.nn.sigmoid(gate) * up).astype(BF16)
            for cols in chunks:
                part = _dot(act, wd_ref[:, cols])
                if first:
                    o_ref[rows, cols] = part
                else:
                    o_ref[rows, cols] += part
            if last:
                o_ref[rows, :] = x_ref[rows, :] + FFN_RES_SCALE * _rms(o_ref[rows, :], post_g_ref[...])

    pl.when(j == 0)(lambda: body(True, False))
    pl.when((j > 0) & (j < last_j))(lambda: body(False, False))
    pl.when(j == last_j)(lambda: body(False, True))


def _ffn(h, pre_g, w_gu, w_down, post_g, to_cast=(), *, tm, tf, sub):
    m, d = h.shape
    d_ff = w_down.shape[0]
    nf = d_ff // tf
    assert m % tm == 0 and tm % sub == 0 and d_ff % tf == 0 and d % tf == 0
    assert nf >= 2, "the first and the last d_ff step must be different steps"
    nc = nf - 1
    assert all(w.shape[0] % (m // tm * BF16_SUBLANES) == 0 and w.shape[1] % (nc * 128) == 0 for w in to_cast)
    blocks = [pl.BlockSpec((w.shape[0] // (m // tm), w.shape[1] // nc), lambda i, j: (i, jnp.minimum(j, nc - 1)))
              for w in to_cast]
    out = pl.pallas_call(
        functools.partial(_ffn_kernel, sub=sub, n_cast=len(to_cast)),
        grid=(m // tm, nf),
        in_specs=[
            pl.BlockSpec((tm, d), lambda i, j: (i, 0)),
            pl.BlockSpec((1, d), lambda i, j: (0, 0)),
            pl.BlockSpec((d, tf), lambda i, j: (0, j)),
            pl.BlockSpec((d, tf), lambda i, j: (0, j + nf)),
            pl.BlockSpec((tf, d), lambda i, j: (j, 0)),
            pl.BlockSpec((1, d), lambda i, j: (0, 0)),
        ] + blocks,
        out_specs=[pl.BlockSpec((tm, d), lambda i, j: (i, 0))] + blocks,
        out_shape=[jax.ShapeDtypeStruct((m, d), F32)] + [jax.ShapeDtypeStruct(w.shape, BF16) for w in to_cast],
        scratch_shapes=[pltpu.VMEM((tm, d), BF16)],
        compiler_params=_params("parallel", "arbitrary"),
        name="ffn",
    )(h, pre_g, w_gu, w_gu, w_down, post_g, *to_cast)
    return out[0], out[1:]


def _in_proj_kernel(x_ref, g_ref, w_ref, cos_ref, sin_ref, y_ref, q_ref, k_ref, v_ref, *, sub):
    width = y_ref.shape[1]
    n_heads = width // HEAD_DIM
    col = lambda n: w_ref[:, n * width:(n + 1) * width]

    for r in range(x_ref.shape[0] // sub):
        rows = slice(r * sub, (r + 1) * sub)
        xn = _rms(x_ref[rows, :], g_ref[...]).astype(BF16)
        cos, sin = cos_ref[rows, :], sin_ref[rows, :]

        def rope(z, o_ref):
            for h in range(n_heads):
                cols = slice(h * HEAD_DIM, (h + 1) * HEAD_DIM)
                zh = z[:, cols]
                o_ref[rows, cols] = zh * cos + pltpu.roll(zh, HEAD_DIM // 2, axis=1) * sin

        y_ref[rows, :] = _dot(xn, col(0)) * jax.nn.sigmoid(_dot(xn, col(1)))
        rope(_dot(xn, col(2)), q_ref)
        rope(_dot(xn, col(3)), k_ref)
        v_ref[rows, :] = _dot(xn, col(4)).astype(BF16)


def _in_proj(h, g, w_in, cos, sin, *, tm, seq, sub):
    m, d = h.shape
    width = w_in.shape[1] // 5
    assert seq % tm == 0 and tm % sub == 0 and width % HEAD_DIM == 0
    row = lambda i: (i, 0)
    table = pl.BlockSpec((tm, HEAD_DIM), lambda i: (i % (seq // tm), 0))
    return pl.pallas_call(
        functools.partial(_in_proj_kernel, sub=sub),
        grid=(m // tm,),
        in_specs=[pl.BlockSpec((tm, d), row), _resident((1, d)), _resident(w_in.shape), table, table],
        out_specs=[pl.BlockSpec((tm, width), row)] * 4,
        out_shape=[
            jax.ShapeDtypeStruct((m, width), F32),
            jax.ShapeDtypeStruct((m, width), F32),
            jax.ShapeDtypeStruct((m, width), F32),
            jax.ShapeDtypeStruct((m, width), BF16),
        ],
        compiler_params=_params("parallel"),
        name="in_proj",
    )(h, g, w_in, cos, sin)


def _conv_kernel(y_ref, w_ref, b_ref, ln_g_ref, ln_b_ref, o_ref, win_ref, shift_ref, w8_ref, conv_ref,
                 *, rows):
    t = pl.program_id(1)
    ts, c = y_ref.shape

    @pl.when(t == 0)
    def _():
        win_ref[0:CONV_PAD, :] = jnp.zeros((CONV_PAD, c), F32)
        for tap in range(CONV_KERNEL):
            w8_ref[tap] = jnp.broadcast_to(w_ref[tap:tap + 1, :], (SUBLANES, c))

    @pl.when(t > 0)
    def _():
        win_ref[0:CONV_PAD, :] = win_ref[ts:ts + CONV_PAD, :]

    win_ref[CONV_PAD:CONV_PAD + ts, :] = y_ref[...]
    span = shift_ref.shape[1]
    for b in range(1, SUBLANES):
        shift_ref[b - 1] = win_ref[b:b + span, :]

    first = CONV_PAD - (CONV_KERNEL - 1)
    n_sub = rows // SUBLANES

    def chunk(ci, carry):
        r0 = pl.multiple_of(ci * rows, rows)
        accs = [jnp.broadcast_to(b_ref[...], (SUBLANES, c))] * n_sub
        for tap in range(CONV_KERNEL):
            off, b = divmod(first + tap, SUBLANES)
            src = win_ref if b == 0 else shift_ref.at[b - 1]
            wt = w8_ref[tap]
            for i in range(n_sub):
                start = pl.multiple_of(r0 + (off + i) * SUBLANES, SUBLANES)
                accs[i] = accs[i] + src[pl.ds(start, SUBLANES), :] * wt
        for i in range(n_sub):
            conv_ref[pl.ds(pl.multiple_of(r0 + i * SUBLANES, SUBLANES), SUBLANES), :] = accs[i]
        return carry

    lax.fori_loop(0, ts // rows, chunk, 0, unroll=2)

    acc = conv_ref[...]
    mu = jnp.mean(acc, axis=-1, keepdims=True)
    xc = acc - mu
    var = jnp.mean(xc * xc, axis=-1, keepdims=True)
    yn = xc * lax.rsqrt(var + LN_EPS) * ln_g_ref[...] + ln_b_ref[...]
    o_ref[...] = (yn * jax.nn.sigmoid(yn)).astype(o_ref.dtype)


def _conv_group(y, w_dw, b_dw, ln_g, ln_b, *, ts, seq, rows):
    m, c = y.shape
    nt = seq // ts
    assert seq % ts == 0 and ts % (2 * rows) == 0 and rows % SUBLANES == 0 and CONV_PAD >= CONV_KERNEL - 1
    const = lambda b, t: (0, 0)
    return pl.pallas_call(
        functools.partial(_conv_kernel, rows=rows),
        grid=(m // seq, nt),
        in_specs=[
            pl.BlockSpec((ts, c), lambda b, t: (b * nt + t, 0)),
            pl.BlockSpec((CONV_KERNEL, c), const),
            pl.BlockSpec((1, c), const),
            pl.BlockSpec((1, c), const),
            pl.BlockSpec((1, c), const),
        ],
        out_specs=pl.BlockSpec((ts, c), lambda b, t: (b * nt + t, 0)),
        out_shape=jax.ShapeDtypeStruct((m, c), BF16),
        scratch_shapes=[
            pltpu.VMEM((CONV_PAD + ts, c), F32),
            pltpu.VMEM((SUBLANES - 1, CONV_PAD + ts - SUBLANES, c), F32),
            pltpu.VMEM((CONV_KERNEL, SUBLANES, c), F32),
            pltpu.VMEM((ts, c), F32),
        ],
        compiler_params=_params("parallel", "arbitrary"),
        name="conv_group",
    )(y, w_dw, b_dw, ln_g, ln_b)


def _moba_kernel(q_ref, k_ref, v_ref, *refs, n_cast):
    cast_in, o_ref, cast_out = refs[:n_cast], refs[n_cast], refs[n_cast + 1:2 * n_cast + 1]
    kb_ref, vt_ref, kmean_ref, p_ref = refs[2 * n_cast + 1:]
    for src, dst in zip(cast_in, cast_out):
        dst[...] = src[...].astype(dst.dtype)

    bs = MOBA_BLOCK
    nb = q_ref.shape[0] // bs
    n_heads = q_ref.shape[1] // HEAD_DIM
    c = HEAD_DIM ** -0.5 * math.log2(math.e)
    head = lambda h: slice(h * HEAD_DIM, (h + 1) * HEAD_DIM)

    for h in range(n_heads):
        for j in range(nb):
            blk_rows = slice(j * bs, (j + 1) * bs)
            kj = k_ref[blk_rows, head(h)]
            kb_ref[h, blk_rows, :] = kj.astype(BF16)
            kmean_ref[h, j:j + 1, :] = jnp.mean(kj, axis=0, keepdims=True)
            vt_ref[h, 0:HEAD_DIM, blk_rows] = v_ref[blk_rows, head(h)].astype(F32).T.astype(BF16)
        vt_ref[h, HEAD_DIM:, :] = jnp.ones((vt_ref.shape[1] - HEAD_DIM, vt_ref.shape[2]), BF16)

    blk = lax.broadcasted_iota(jnp.int32, (nb, bs), 0)
    causal = (lax.broadcasted_iota(jnp.int32, (bs, bs), 0)
              <= lax.broadcasted_iota(jnp.int32, (bs, bs), 1))

    def scores(h, qi):
        qt = q_ref[qi * bs:(qi + 1) * bs, head(h)].T
        s = _dot(kb_ref[h, 0:(qi + 1) * bs, :], (qt * c).astype(BF16))
        keep = None
        if qi > MOBA_TOPK:
            gate = jnp.dot(kmean_ref[h], qt, precision=lax.Precision.HIGHEST, preferred_element_type=F32)
            beaten_by = jnp.zeros((nb, bs), jnp.int32)
            for i in range(qi):
                gi = gate[i:i + 1, :]
                beats = (gi > gate) | ((gi == gate) & (i < blk))
                beaten_by = beaten_by + beats.astype(jnp.int32)
            keep = jnp.where((blk < qi) & (beaten_by < MOBA_TOPK), 1.0, 0.0)
        return s, keep

    def softmax(slot, qi, s, keep):
        blocks = []
        for j in range(qi + 1):
            sj = s[j * bs:(j + 1) * bs, :]
            if j == qi:
                sj = jnp.where(causal, sj, NEG_INF)
            elif keep is not None:
                sj = jnp.where(keep[j:j + 1, :] > 0.5, sj, NEG_INF)
            blocks.append(sj)
        m = functools.reduce(jnp.maximum, [jnp.max(sj, axis=0, keepdims=True) for sj in blocks])
        for j, sj in enumerate(blocks):
            p_ref[slot, j * bs:(j + 1) * bs, :] = jnp.exp2(sj - m).astype(BF16)

    def weighted_values(slot, h, qi):
        nk = (qi + 1) * bs
        acc = _dot(vt_ref[h, :, 0:nk], p_ref[slot, 0:nk, :])
        out = acc[0:HEAD_DIM, :] / acc[HEAD_DIM:HEAD_DIM + 1, :]
        o_ref[qi * bs:(qi + 1) * bs, head(h)] = out.T.astype(o_ref.dtype)

    items = [(h, qi) for qi in range(nb - 1, -1, -1) for h in range(n_heads)]
    scored = {}
    for step in range(len(items) + 2):
        if step < len(items):
            scored[step] = scores(*items[step])
        if 0 <= step - 1 < len(items):
            softmax((step - 1) % 2, items[step - 1][1], *scored.pop(step - 1))
        if 0 <= step - 2 < len(items):
            weighted_values((step - 2) % 2, *items[step - 2])


def _moba(q, k, v, to_cast, *, seq, heads_per_step):
    m, width = q.shape
    cols = heads_per_step * HEAD_DIM
    assert seq % MOBA_BLOCK == 0 and width % cols == 0
    groups = width // cols
    steps = (m // seq) * groups
    assert all(w.shape[0] % (steps * BF16_SUBLANES) == 0 for w in to_cast)
    spec = pl.BlockSpec((seq, cols), lambda b, h: (b, h))
    slabs = [pl.BlockSpec((w.shape[0] // steps, w.shape[1]), lambda b, h: (b * groups + h, 0)) for w in to_cast]
    out = pl.pallas_call(
        functools.partial(_moba_kernel, n_cast=len(to_cast)),
        grid=(m // seq, groups),
        in_specs=[spec, spec, spec] + slabs,
        out_specs=[spec] + slabs,
        out_shape=[jax.ShapeDtypeStruct((m, width), BF16)] + [jax.ShapeDtypeStruct(w.shape, BF16) for w in to_cast],
        scratch_shapes=[
            pltpu.VMEM((heads_per_step, seq, HEAD_DIM), BF16),
            pltpu.VMEM((heads_per_step, HEAD_DIM + BF16_SUBLANES, seq), BF16),
            pltpu.VMEM((heads_per_step, seq // MOBA_BLOCK, HEAD_DIM), F32),
            pltpu.VMEM((2, seq, MOBA_BLOCK), BF16),
        ],
        compiler_params=_params("parallel", "parallel"),
        name="moba",
    )(q, k, v, *to_cast)
    return out[0], out[1:]


def _norm_matmul_kernel(x_ref, g_ref, w_ref, o_ref):
    xn = _rms(x_ref[...], g_ref[...]).astype(BF16)
    o_ref[...] = _dot(xn, w_ref[...]).astype(o_ref.dtype)


def _norm_matmul(x, g, w, *, tm, out_dtype):
    m, d = x.shape
    n = w.shape[1]
    return pl.pallas_call(
        _norm_matmul_kernel,
        grid=(m // tm,),
        in_specs=[pl.BlockSpec((tm, d), lambda i: (i, 0)), _resident((1, d)), _resident((d, n))],
        out_specs=pl.BlockSpec((tm, n), lambda i: (i, 0)),
        out_shape=jax.ShapeDtypeStruct((m, n), out_dtype),
        compiler_params=_params("parallel"),
        name="mem_kv",
    )(x, g, w)


def _mix_out_kernel(conv_ref, attn_ref, h_ref, w_out_ref, mix_g_ref,
                    xpre_g_ref, wq_ref, kv_ref, wo_ref, xpost_g_ref, o_ref, *, sub):
    cw = conv_ref.shape[1]
    width = wq_ref.shape[1]
    scale = HEAD_DIM ** -0.5
    n_sub = h_ref.shape[0] // sub
    tile = lambda r: slice(r * sub, (r + 1) * sub)

    def mix(r):
        return _dot(conv_ref[tile(r), :], w_out_ref[0:cw, :]) + _dot(attn_ref[tile(r), :], w_out_ref[cw:, :])

    def query(r, y):
        h = h_ref[tile(r), :] + _rms(y, mix_g_ref[...])
        return h, _dot(_rms(h, xpre_g_ref[...]).astype(BF16), wq_ref[...])

    def attend(r, hq):
        h, q = hq
        heads = []
        for hd in range(N_XATTN_HEADS):
            cols = slice(hd * HEAD_DIM, (hd + 1) * HEAD_DIM)
            vcols = slice(width + hd * HEAD_DIM, width + (hd + 1) * HEAD_DIM)
            s = lax.dot_general(q[:, cols].astype(BF16), kv_ref[:, cols], _NT,
                                preferred_element_type=F32) * scale
            e = jnp.exp(s - jnp.max(s, axis=-1, keepdims=True))
            p = e / jnp.sum(e, axis=-1, keepdims=True)
            heads.append(_dot(p.astype(BF16), kv_ref[:, vcols]))
        return h, jnp.concatenate(heads, axis=1).astype(BF16)

    def project(r, ho):
        h, o = ho
        o_ref[tile(r), :] = h + _rms(_dot(o, wo_ref[...]), xpost_g_ref[...])

    stages = [lambda r, _: mix(r), query, attend, project]
    carried = {}
    for step in range(n_sub + len(stages) - 1):
        for k, stage in enumerate(stages):
            r = step - k
            if 0 <= r < n_sub:
                carried[r] = stage(r, carried.get(r))


def _mix_out(conv, attn, h, w_out, mix_g, xpre_g, w_q, kv, w_o, xpost_g, *, tm, seq, sub):
    m, d = h.shape
    mem_len = kv.shape[0] // (m // seq)
    assert seq % tm == 0 and tm % sub == 0 and w_out.shape[0] == conv.shape[1] + attn.shape[1]
    row = lambda i: (i, 0)
    return pl.pallas_call(
        functools.partial(_mix_out_kernel, sub=sub),
        grid=(m // tm,),
        in_specs=[
            pl.BlockSpec((tm, conv.shape[1]), row),
            pl.BlockSpec((tm, attn.shape[1]), row),
            pl.BlockSpec((tm, d), row),
            _resident(w_out.shape),
            _resident((1, d)),
            _resident((1, d)),
            _resident(w_q.shape),
            pl.BlockSpec((mem_len, kv.shape[1]), lambda i: (i // (seq // tm), 0)),
            _resident(w_o.shape),
            _resident((1, d)),
        ],
        out_specs=pl.BlockSpec((tm, d), row),
        out_shape=jax.ShapeDtypeStruct((m, d), F32),
        compiler_params=_params("parallel"),
        name="mix_out",
    )(conv, attn, h, w_out, mix_g, xpre_g, w_q, kv, w_o, xpost_g)


def _rope_tables(seq):
    inv_freq = 1.0 / (ROPE_THETA ** (jnp.arange(0, HEAD_DIM, 2, dtype=F32) / HEAD_DIM))
    ang = jnp.arange(seq, dtype=F32)[:, None] * inv_freq[None, :]
    cos, sin = jnp.cos(ang), jnp.sin(ang)
    return jnp.concatenate([cos, cos], axis=-1), jnp.concatenate([-sin, sin], axis=-1)


def kernel(x, mem, ffn1_pre_g, ffn1_w_gu, ffn1_w_down, ffn1_post_g, mix_pre_g, w_in, conv_w_dw, conv_b_dw, conv_ln_g, conv_ln_b, w_out, mix_post_g, xattn_pre_g, mem_g, xattn_w_q, xattn_w_kv, xattn_w_o, xattn_post_g, ffn2_pre_g, ffn2_w_gu, ffn2_w_down, ffn2_post_g):
    batch, seq, d = x.shape
    mem_len = mem.shape[1]
    cos, sin = _rope_tables(seq)
    row = lambda a, l: a[l][None, :]
    h = x.reshape(batch * seq, d)
    mem2d = mem.reshape(batch * mem_len, d)

    for l in range(ffn1_w_gu.shape[0]):
        h, (w_in_b,) = _ffn(h, row(ffn1_pre_g, l), ffn1_w_gu[l].astype(BF16), ffn1_w_down[l].astype(BF16),
                            row(ffn1_post_g, l), (w_in[l],), tm=FFN_ROWS, tf=FFN_COLS, sub=FFN_SUB_ROWS)

        y, q, k, v = _in_proj(h, row(mix_pre_g, l), w_in_b, cos, sin,
                              tm=IN_PROJ_ROWS, seq=seq, sub=IN_PROJ_SUB_ROWS)
        conv = _conv_group(y, conv_w_dw[l], row(conv_b_dw, l), row(conv_ln_g, l), row(conv_ln_b, l),
                           ts=CONV_ROWS, seq=seq, rows=CONV_CHUNK_ROWS)
        attn, (w_out_b, w_q_b, w_kv_b, w_o_b, ffn2_w_gu_b, ffn2_w_down_b) = _moba(
            q, k, v, (w_out[l], xattn_w_q[l], xattn_w_kv[l], xattn_w_o[l], ffn2_w_gu[l], ffn2_w_down[l]),
            seq=seq, heads_per_step=MOBA_HEADS_PER_STEP)

        kv = _norm_matmul(mem2d, row(mem_g, l), w_kv_b, tm=MEM_KV_ROWS, out_dtype=BF16)
        h = _mix_out(conv, attn, h, w_out_b, row(mix_post_g, l), row(xattn_pre_g, l),
                     w_q_b, kv, w_o_b, row(xattn_post_g, l),
                     tm=MIX_OUT_ROWS, seq=seq, sub=MIX_OUT_SUB_ROWS)

        h, _ = _ffn(h, row(ffn2_pre_g, l), ffn2_w_gu_b, ffn2_w_down_b,
                    row(ffn2_post_g, l), tm=FFN_ROWS, tf=FFN_COLS, sub=FFN_SUB_ROWS)
    return h.reshape(batch, seq, d)
```

```python
import functools
import math

import jax
import jax.numpy as jnp
from jax import lax
from jax.experimental import pallas as pl
from jax.experimental.pallas import tpu as pltpu

F32 = jnp.float32
BF16 = jnp.bfloat16

HEAD_DIM = 128
CONV_KERNEL = 31
MOBA_BLOCK = 256
MOBA_TOPK = 3
N_XATTN_HEADS = 4
ROPE_THETA = 10000.0
RMS_EPS = 1e-6
LN_EPS = 1e-5
FFN_RES_SCALE = 0.5
NEG_INF = -1e30

V7X_VMEM_BYTES = 64 * 1024 * 1024
VMEM_LIMIT_BYTES = V7X_VMEM_BYTES - 3 * 1024 * 1024
SUBLANES = 8
BF16_SUBLANES = 16
CONV_PAD = 32

_NT = (((1,), (1,)), ((), ()))

FFN_ROWS, FFN_SUB_ROWS, FFN_COLS = 1024, 512, 512
IN_PROJ_ROWS, IN_PROJ_SUB_ROWS = 512, 256
CONV_ROWS, CONV_CHUNK_ROWS = 512, 32
MIX_OUT_ROWS, MIX_OUT_SUB_ROWS = 512, 256
MEM_KV_ROWS = 256
MOBA_HEADS_PER_STEP = 2


def _params(*semantics):
    return pltpu.CompilerParams(dimension_semantics=semantics, vmem_limit_bytes=VMEM_LIMIT_BYTES)


def _resident(shape):
    return pl.BlockSpec(shape, lambda *_: (0,) * len(shape), pipeline_mode=pl.Buffered(1))


def _rms(x, g):
    return x * lax.rsqrt(jnp.mean(x * x, axis=-1, keepdims=True) + RMS_EPS) * g


def _dot(a, b):
    return jnp.dot(a, b, preferred_element_type=F32)


def _ffn_kernel(x_ref, pre_g_ref, wg_ref, wu_ref, wd_ref, post_g_ref, *refs, sub, n_cast):
    cast_in, o_ref, cast_out, xn_ref = refs[:n_cast], refs[n_cast], refs[n_cast + 1:2 * n_cast + 1], refs[-1]
    j = pl.program_id(1)
    last_j = pl.num_programs(1) - 1
    tiles = [slice(r * sub, (r + 1) * sub) for r in range(o_ref.shape[0] // sub)]
    tf = wd_ref.shape[0]
    chunks = [slice(n * tf, (n + 1) * tf) for n in range(o_ref.shape[1] // tf)]

    def body(first, last):
        for src, dst in zip(cast_in, cast_out):
            dst[...] = src[...].astype(dst.dtype)
        for rows in tiles:
            if first:
                xn = _rms(x_ref[rows, :], pre_g_ref[...]).astype(BF16)
                xn_ref[rows, :] = xn
            else:
                xn = xn_ref[rows, :]
            gate = _dot(xn, wg_ref[...])
            up = _dot(xn, wu_ref[...])
            act = (gate * jax.nn.sigmoid(gate) * up).astype(BF16)
            for cols in chunks:
                part = _dot(act, wd_ref[:, cols])
                if first:
                    o_ref[rows, cols] = part
                else:
                    o_ref[rows, cols] += part
            if last:
                o_ref[rows, :] = x_ref[rows, :] + FFN_RES_SCALE * _rms(o_ref[rows, :], post_g_ref[...])

    pl.when(j == 0)(lambda: body(True, False))
    pl.when((j > 0) & (j < last_j))(lambda: body(False, False))
    pl.when(j == last_j)(lambda: body(False, True))


def _ffn(h, pre_g, w_gu, w_down, post_g, to_cast=(), *, tm, tf, sub):
    m, d = h.shape
    d_ff = w_down.shape[0]
    nf = d_ff // tf
    assert m % tm == 0 and tm % sub == 0 and d_ff % tf == 0 and d % tf == 0
    assert nf >= 2, "the first and the last d_ff step must be different steps"
    nc = nf - 1
    assert all(w.shape[0] % (m // tm * BF16_SUBLANES) == 0 and w.shape[1] % (nc * 128) == 0 for w in to_cast)
    blocks = [pl.BlockSpec((w.shape[0] // (m // tm), w.shape[1] // nc), lambda i, j: (i, jnp.minimum(j, nc - 1)))
              for w in to_cast]
    out = pl.pallas_call(
        functools.partial(_ffn_kernel, sub=sub, n_cast=len(to_cast)),
        grid=(m // tm, nf),
        in_specs=[
            pl.BlockSpec((tm, d), lambda i, j: (i, 0)),
            pl.BlockSpec((1, d), lambda i, j: (0, 0)),
            pl.BlockSpec((d, tf), lambda i, j: (0, j)),
            pl.BlockSpec((d, tf), lambda i, j: (0, j + nf)),
            pl.BlockSpec((tf, d), lambda i, j: (j, 0)),
            pl.BlockSpec((1, d), lambda i, j: (0, 0)),
        ] + blocks,
        out_specs=[pl.BlockSpec((tm, d), lambda i, j: (i, 0))] + blocks,
        out_shape=[jax.ShapeDtypeStruct((m, d), F32)] + [jax.ShapeDtypeStruct(w.shape, BF16) for w in to_cast],
        scratch_shapes=[pltpu.VMEM((tm, d), BF16)],
        compiler_params=_params("parallel", "arbitrary"),
        name="ffn",
    )(h, pre_g, w_gu, w_gu, w_down, post_g, *to_cast)
    return out[0], out[1:]


def _in_proj_kernel(x_ref, g_ref, w_ref, cos_ref, sin_ref, y_ref, q_ref, k_ref, v_ref, *, sub):
    width = y_ref.shape[1]
    n_heads = width // HEAD_DIM
    col = lambda n: w_ref[:, n * width:(n + 1) * width]

    for r in range(x_ref.shape[0] // sub):
        rows = slice(r * sub, (r + 1) * sub)
        xn = _rms(x_ref[rows, :], g_ref[...]).astype(BF16)
        cos, sin = cos_ref[rows, :], sin_ref[rows, :]

        def rope(z, o_ref):
            for h in range(n_heads):
                cols = slice(h * HEAD_DIM, (h + 1) * HEAD_DIM)
                zh = z[:, cols]
                o_ref[rows, cols] = zh * cos + pltpu.roll(zh, HEAD_DIM // 2, axis=1) * sin

        y_ref[rows, :] = _dot(xn, col(0)) * jax.nn.sigmoid(_dot(xn, col(1)))
        rope(_dot(xn, col(2)), q_ref)
        rope(_dot(xn, col(3)), k_ref)
        v_ref[rows, :] = _dot(xn, col(4)).astype(BF16)


def _in_proj(h, g, w_in, cos, sin, *, tm, seq, sub):
    m, d = h.shape
    width = w_in.shape[1] // 5
    assert seq % tm == 0 and tm % sub == 0 and width % HEAD_DIM == 0
    row = lambda i: (i, 0)
    table = pl.BlockSpec((tm, HEAD_DIM), lambda i: (i % (seq // tm), 0))
    return pl.pallas_call(
        functools.partial(_in_proj_kernel, sub=sub),
        grid=(m // tm,),
        in_specs=[pl.BlockSpec((tm, d), row), _resident((1, d)), _resident(w_in.shape), table, table],
        out_specs=[pl.BlockSpec((tm, width), row)] * 4,
        out_shape=[
            jax.ShapeDtypeStruct((m, width), F32),
            jax.ShapeDtypeStruct((m, width), F32),
            jax.ShapeDtypeStruct((m, width), F32),
            jax.ShapeDtypeStruct((m, width), BF16),
        ],
        compiler_params=_params("parallel"),
        name="in_proj",
    )(h, g, w_in, cos, sin)


def _conv_kernel(y_ref, w_ref, b_ref, ln_g_ref, ln_b_ref, o_ref, win_ref, shift_ref, w8_ref, conv_ref,
                 *, rows):
    t = pl.program_id(1)
    ts, c = y_ref.shape

    @pl.when(t == 0)
    def _():
        win_ref[0:CONV_PAD, :] = jnp.zeros((CONV_PAD, c), F32)
        for tap in range(CONV_KERNEL):
            w8_ref[tap] = jnp.broadcast_to(w_ref[tap:tap + 1, :], (SUBLANES, c))

    @pl.when(t > 0)
    def _():
        win_ref[0:CONV_PAD, :] = win_ref[ts:ts + CONV_PAD, :]

    win_ref[CONV_PAD:CONV_PAD + ts, :] = y_ref[...]
    span = shift_ref.shape[1]
    for b in range(1, SUBLANES):
        shift_ref[b - 1] = win_ref[b:b + span, :]

    first = CONV_PAD - (CONV_KERNEL - 1)
    n_sub = rows // SUBLANES

    def chunk(ci, carry):
        r0 = pl.multiple_of(ci * rows, rows)
        accs = [jnp.broadcast_to(b_ref[...], (SUBLANES, c))] * n_sub
        for tap in range(CONV_KERNEL):
            off, b = divmod(first + tap, SUBLANES)
            src = win_ref if b == 0 else shift_ref.at[b - 1]
            wt = w8_ref[tap]
            for i in range(n_sub):
                start = pl.multiple_of(r0 + (off + i) * SUBLANES, SUBLANES)
                accs[i] = accs[i] + src[pl.ds(start, SUBLANES), :] * wt
        for i in range(n_sub):
            conv_ref[pl.ds(pl.multiple_of(r0 + i * SUBLANES, SUBLANES), SUBLANES), :] = accs[i]
        return carry

    lax.fori_loop(0, ts // rows, chunk, 0, unroll=2)

    acc = conv_ref[...]
    mu = jnp.mean(acc, axis=-1, keepdims=True)
    xc = acc - mu
    var = jnp.mean(xc * xc, axis=-1, keepdims=True)
    yn = xc * lax.rsqrt(var + LN_EPS) * ln_g_ref[...] + ln_b_ref[...]
    o_ref[...] = (yn * jax.nn.sigmoid(yn)).astype(o_ref.dtype)


def _conv_group(y, w_dw, b_dw, ln_g, ln_b, *, ts, seq, rows):
    m, c = y.shape
    nt = seq // ts
    assert seq % ts == 0 and ts % (2 * rows) == 0 and rows % SUBLANES == 0 and CONV_PAD >= CONV_KERNEL - 1
    const = lambda b, t: (0, 0)
    return pl.pallas_call(
        functools.partial(_conv_kernel, rows=rows),
        grid=(m // seq, nt),
        in_specs=[
            pl.BlockSpec((ts, c), lambda b, t: (b * nt + t, 0)),
            pl.BlockSpec((CONV_KERNEL, c), const),
            pl.BlockSpec((1, c), const),
            pl.BlockSpec((1, c), const),
            pl.BlockSpec((1, c), const),
        ],
        out_specs=pl.BlockSpec((ts, c), lambda b, t: (b * nt + t, 0)),
        out_shape=jax.ShapeDtypeStruct((m, c), BF16),
        scratch_shapes=[
            pltpu.VMEM((CONV_PAD + ts, c), F32),
            pltpu.VMEM((SUBLANES - 1, CONV_PAD + ts - SUBLANES, c), F32),
            pltpu.VMEM((CONV_KERNEL, SUBLANES, c), F32),
            pltpu.VMEM((ts, c), F32),
        ],
        compiler_params=_params("parallel", "arbitrary"),
        name="conv_group",
    )(y, w_dw, b_dw, ln_g, ln_b)


def _moba_kernel(q_ref, k_ref, v_ref, *refs, n_cast):
    cast_in, o_ref, cast_out = refs[:n_cast], refs[n_cast], refs[n_cast + 1:2 * n_cast + 1]
    kb_ref, vt_ref, kmean_ref, p_ref = refs[2 * n_cast + 1:]
    for src, dst in zip(cast_in, cast_out):
        dst[...] = src[...].astype(dst.dtype)

    bs = MOBA_BLOCK
    nb = q_ref.shape[0] // bs
    n_heads = q_ref.shape[1] // HEAD_DIM
    c = HEAD_DIM ** -0.5 * math.log2(math.e)
    head = lambda h: slice(h * HEAD_DIM, (h + 1) * HEAD_DIM)

    for h in range(n_heads):
        for j in range(nb):
            blk_rows = slice(j * bs, (j + 1) * bs)
            kj = k_ref[blk_rows, head(h)]
            kb_ref[h, blk_rows, :] = kj.astype(BF16)
            kmean_ref[h, j:j + 1, :] = jnp.mean(kj, axis=0, keepdims=True)
            vt_ref[h, 0:HEAD_DIM, blk_rows] = v_ref[blk_rows, head(h)].astype(F32).T.astype(BF16)
        vt_ref[h, HEAD_DIM:, :] = jnp.ones((vt_ref.shape[1] - HEAD_DIM, vt_ref.shape[2]), BF16)

    blk = lax.broadcasted_iota(jnp.int32, (nb, bs), 0)
    causal = (lax.broadcasted_iota(jnp.int32, (bs, bs), 0)
              <= lax.broadcasted_iota(jnp.int32, (bs, bs), 1))

    def scores(h, qi):
        qt = q_ref[qi * bs:(qi + 1) * bs, head(h)].T
        s = _dot(kb_ref[h, 0:(qi + 1) * bs, :], (qt * c).astype(BF16))
        keep = None
        if qi > MOBA_TOPK:
            gate = jnp.dot(kmean_ref[h], qt, precision=lax.Precision.HIGHEST, preferred_element_type=F32)
            beaten_by = jnp.zeros((nb, bs), jnp.int32)
            for i in range(qi):
                gi = gate[i:i + 1, :]
                beats = (gi > gate) | ((gi == gate) & (i < blk))
                beaten_by = beaten_by + beats.astype(jnp.int32)
            keep = jnp.where((blk < qi) & (beaten_by < MOBA_TOPK), 1.0, 0.0)
        return s, keep

    def softmax(slot, qi, s, keep):
        blocks = []
        for j in range(qi + 1):
            sj = s[j * bs:(j + 1) * bs, :]
            if j == qi:
                sj = jnp.where(causal, sj, NEG_INF)
            elif keep is not None:
                sj = jnp.where(keep[j:j + 1, :] > 0.5, sj, NEG_INF)
            blocks.append(sj)
        m = functools.reduce(jnp.maximum, [jnp.max(sj, axis=0, keepdims=True) for sj in blocks])
        for j, sj in enumerate(blocks):
            p_ref[slot, j * bs:(j + 1) * bs, :] = jnp.exp2(sj - m).astype(BF16)

    def weighted_values(slot, h, qi):
        nk = (qi + 1) * bs
        acc = _dot(vt_ref[h, :, 0:nk], p_ref[slot, 0:nk, :])
        out = acc[0:HEAD_DIM, :] / acc[HEAD_DIM:HEAD_DIM + 1, :]
        o_ref[qi * bs:(qi + 1) * bs, head(h)] = out.T.astype(o_ref.dtype)

    items = [(h, qi) for qi in range(nb - 1, -1, -1) for h in range(n_heads)]
    scored = {}
    for step in range(len(items) + 2):
        if step < len(items):
            scored[step] = scores(*items[step])
        if 0 <= step - 1 < len(items):
            softmax((step - 1) % 2, items[step - 1][1], *scored.pop(step - 1))
        if 0 <= step - 2 < len(items):
            weighted_values((step - 2) % 2, *items[step - 2])


def _moba(q, k, v, to_cast, *, seq, heads_per_step):
    m, width = q.shape
    cols = heads_per_step * HEAD_DIM
    assert seq % MOBA_BLOCK == 0 and width % cols == 0
    groups = width // cols
    steps = (m // seq) * groups
    assert all(w.shape[0] % (steps * BF16_SUBLANES) == 0 for w in to_cast)
    spec = pl.BlockSpec((seq, cols), lambda b, h: (b, h))
    slabs = [pl.BlockSpec((w.shape[0] // steps, w.shape[1]), lambda b, h: (b * groups + h, 0)) for w in to_cast]
    out = pl.pallas_call(
        functools.partial(_moba_kernel, n_cast=len(to_cast)),
        grid=(m // seq, groups),
        in_specs=[spec, spec, spec] + slabs,
        out_specs=[spec] + slabs,
        out_shape=[jax.ShapeDtypeStruct((m, width), BF16)] + [jax.ShapeDtypeStruct(w.shape, BF16) for w in to_cast],
        scratch_shapes=[
            pltpu.VMEM((heads_per_step, seq, HEAD_DIM), BF16),
            pltpu.VMEM((heads_per_step, HEAD_DIM + BF16_SUBLANES, seq), BF16),
            pltpu.VMEM((heads_per_step, seq // MOBA_BLOCK, HEAD_DIM), F32),
            pltpu.VMEM((2, seq, MOBA_BLOCK), BF16),
        ],
        compiler_params=_params("parallel", "parallel"),
        name="moba",
    )(q, k, v, *to_cast)
    return out[0], out[1:]


def _norm_matmul_kernel(x_ref, g_ref, w_ref, o_ref):
    xn = _rms(x_ref[...], g_ref[...]).astype(BF16)
    o_ref[...] = _dot(xn, w_ref[...]).astype(o_ref.dtype)


def _norm_matmul(x, g, w, *, tm, out_dtype):
    m, d = x.shape
    n = w.shape[1]
    return pl.pallas_call(
        _norm_matmul_kernel,
        grid=(m // tm,),
        in_specs=[pl.BlockSpec((tm, d), lambda i: (i, 0)), _resident((1, d)), _resident((d, n))],
        out_specs=pl.BlockSpec((tm, n), lambda i: (i, 0)),
        out_shape=jax.ShapeDtypeStruct((m, n), out_dtype),
        compiler_params=_params("parallel"),
        name="mem_kv",
    )(x, g, w)


def _mix_out_kernel(conv_ref, attn_ref, h_ref, w_out_ref, mix_g_ref,
                    xpre_g_ref, wq_ref, kv_ref, wo_ref, xpost_g_ref, o_ref, *, sub):
    cw = conv_ref.shape[1]
    width = wq_ref.shape[1]
    scale = HEAD_DIM ** -0.5
    n_sub = h_ref.shape[0] // sub
    tile = lambda r: slice(r * sub, (r + 1) * sub)

    def mix(r):
        return _dot(conv_ref[tile(r), :], w_out_ref[0:cw, :]) + _dot(attn_ref[tile(r), :], w_out_ref[cw:, :])

    def query(r, y):
        h = h_ref[tile(r), :] + _rms(y, mix_g_ref[...])
        return h, _dot(_rms(h, xpre_g_ref[...]).astype(BF16), wq_ref[...])

    def attend(r, hq):
        h, q = hq
        heads = []
        for hd in range(N_XATTN_HEADS):
            cols = slice(hd * HEAD_DIM, (hd + 1) * HEAD_DIM)
            vcols = slice(width + hd * HEAD_DIM, width + (hd + 1) * HEAD_DIM)
            s = lax.dot_general(q[:, cols].astype(BF16), kv_ref[:, cols], _NT,
                                preferred_element_type=F32) * scale
            e = jnp.exp(s - jnp.max(s, axis=-1, keepdims=True))
            p = e / jnp.sum(e, axis=-1, keepdims=True)
            heads.append(_dot(p.astype(BF16), kv_ref[:, vcols]))
        return h, jnp.concatenate(heads, axis=1).astype(BF16)

    def project(r, ho):
        h, o = ho
        o_ref[tile(r), :] = h + _rms(_dot(o, wo_ref[...]), xpost_g_ref[...])

    stages = [lambda r, _: mix(r), query, attend, project]
    carried = {}
    for step in range(n_sub + len(stages) - 1):
        for k, stage in enumerate(stages):
            r = step - k
            if 0 <= r < n_sub:
                carried[r] = stage(r, carried.get(r))


def _mix_out(conv, attn, h, w_out, mix_g, xpre_g, w_q, kv, w_o, xpost_g, *, tm, seq, sub):
    m, d = h.shape
    mem_len = kv.shape[0] // (m // seq)
    assert seq % tm == 0 and tm % sub == 0 and w_out.shape[0] == conv.shape[1] + attn.shape[1]
    row = lambda i: (i, 0)
    return pl.pallas_call(
        functools.partial(_mix_out_kernel, sub=sub),
        grid=(m // tm,),
        in_specs=[
            pl.BlockSpec((tm, conv.shape[1]), row),
            pl.BlockSpec((tm, attn.shape[1]), row),
            pl.BlockSpec((tm, d), row),
            _resident(w_out.shape),
            _resident((1, d)),
            _resident((1, d)),
            _resident(w_q.shape),
            pl.BlockSpec((mem_len, kv.shape[1]), lambda i: (i // (seq // tm), 0)),
            _resident(w_o.shape),
            _resident((1, d)),
        ],
        out_specs=pl.BlockSpec((tm, d), row),
        out_shape=jax.ShapeDtypeStruct((m, d), F32),
        compiler_params=_params("parallel"),
        name="mix_out",
    )(conv, attn, h, w_out, mix_g, xpre_g, w_q, kv, w_o, xpost_g)


def _rope_tables(seq):
    inv_freq = 1.0 / (ROPE_THETA ** (jnp.arange(0, HEAD_DIM, 2, dtype=F32) / HEAD_DIM))
    ang = jnp.arange(seq, dtype=F32)[:, None] * inv_freq[None, :]
    cos, sin = jnp.cos(ang), jnp.sin(ang)
    return jnp.concatenate([cos, cos], axis=-1), jnp.concatenate([-sin, sin], axis=-1)


def kernel(x, mem, ffn1_pre_g, ffn1_w_gu, ffn1_w_down, ffn1_post_g, mix_pre_g, w_in, conv_w_dw, conv_b_dw, conv_ln_g, conv_ln_b, w_out, mix_post_g, xattn_pre_g, mem_g, xattn_w_q, xattn_w_kv, xattn_w_o, xattn_post_g, ffn2_pre_g, ffn2_w_gu, ffn2_w_down, ffn2_post_g):
    batch, seq, d = x.shape
    mem_len = mem.shape[1]
    cos, sin = _rope_tables(seq)
    row = lambda a, l: a[l][None, :]
    h = x.reshape(batch * seq, d)
    mem2d = mem.reshape(batch * mem_len, d)

    for l in range(ffn1_w_gu.shape[0]):
        h, (w_in_b,) = _ffn(h, row(ffn1_pre_g, l), ffn1_w_gu[l].astype(BF16), ffn1_w_down[l].astype(BF16),
                            row(ffn1_post_g, l), (w_in[l],), tm=FFN_ROWS, tf=FFN_COLS, sub=FFN_SUB_ROWS)

        y, q, k, v = _in_proj(h, row(mix_pre_g, l), w_in_b, cos, sin,
                              tm=IN_PROJ_ROWS, seq=seq, sub=IN_PROJ_SUB_ROWS)
        conv = _conv_group(y, conv_w_dw[l], row(conv_b_dw, l), row(conv_ln_g, l), row(conv_ln_b, l),
                           ts=CONV_ROWS, seq=seq, rows=CONV_CHUNK_ROWS)
        attn, (w_out_b, w_q_b, w_kv_b, w_o_b, ffn2_w_gu_b, ffn2_w_down_b) = _moba(
            q, k, v, (w_out[l], xattn_w_q[l], xattn_w_kv[l], xattn_w_o[l], ffn2_w_gu[l], ffn2_w_down[l]),
            seq=seq, heads_per_step=MOBA_HEADS_PER_STEP)

        kv = _norm_matmul(mem2d, row(mem_g, l), w_kv_b, tm=MEM_KV_ROWS, out_dtype=BF16)
        h = _mix_out(conv, attn, h, w_out_b, row(mix_post_g, l), row(xattn_pre_g, l),
                     w_q_b, kv, w_o_b, row(xattn_post_g, l),
                     tm=MIX_OUT_ROWS, seq=seq, sub=MIX_OUT_SUB_ROWS)

        h, _ = _ffn(h, row(ffn2_pre_g, l), ffn2_w_gu_b, ffn2_w_down_b,
                    row(ffn2_post_g, l), tm=FFN_ROWS, tf=FFN_COLS, sub=FFN_SUB_ROWS)
    return h.reshape(batch, seq, d)
```

```python
import functools
import math

import jax
import jax.numpy as jnp
from jax import lax
from jax.experimental import pallas as pl
from jax.experimental.pallas import tpu as pltpu

F32 = jnp.float32
BF16 = jnp.bfloat16

HEAD_DIM = 128
CONV_KERNEL = 31
MOBA_BLOCK = 256
MOBA_TOPK = 3
N_XATTN_HEADS = 4
ROPE_THETA = 10000.0
RMS_EPS = 1e-6
LN_EPS = 1e-5
FFN_RES_SCALE = 0.5
NEG_INF = -1e30

V7X_VMEM_BYTES = 64 * 1024 * 1024
VMEM_LIMIT_BYTES = V7X_VMEM_BYTES - 3 * 1024 * 1024
SUBLANES = 8
BF16_SUBLANES = 16
CONV_PAD = 32

_NT = (((1,), (1,)), ((), ()))

FFN_ROWS, FFN_SUB_ROWS, FFN_COLS = 1024, 512, 512
IN_PROJ_ROWS, IN_PROJ_SUB_ROWS = 512, 256
CONV_ROWS, CONV_CHUNK_ROWS = 512, 32
MIX_OUT_ROWS, MIX_OUT_SUB_ROWS = 512, 256
MOBA_HEADS_PER_STEP = 2


def _params(*semantics):
    return pltpu.CompilerParams(dimension_semantics=semantics, vmem_limit_bytes=VMEM_LIMIT_BYTES)


def _resident(shape):
    return pl.BlockSpec(shape, lambda *_: (0,) * len(shape), pipeline_mode=pl.Buffered(1))


def _rms(x, g):
    return x * lax.rsqrt(jnp.mean(x * x, axis=-1, keepdims=True) + RMS_EPS) * g


def _dot(a, b):
    return jnp.dot(a, b, preferred_element_type=F32)


def _ffn_kernel(x_ref, pre_g_ref, wg_ref, wu_ref, wd_ref, post_g_ref, *refs, sub, n_cast):
    cast_in, o_ref, cast_out, xn_ref = refs[:n_cast], refs[n_cast], refs[n_cast + 1:2 * n_cast + 1], refs[-1]
    j = pl.program_id(1)
    last_j = pl.num_programs(1) - 1
    tiles = [slice(r * sub, (r + 1) * sub) for r in range(o_ref.shape[0] // sub)]
    tf = wd_ref.shape[0]
    chunks = [slice(n * tf, (n + 1) * tf) for n in range(o_ref.shape[1] // tf)]

    def body(first, last):
        for src, dst in zip(cast_in, cast_out):
            dst[...] = src[...].astype(dst.dtype)
        for rows in tiles:
            if first:
                xn = _rms(x_ref[rows, :], pre_g_ref[...]).astype(BF16)
                xn_ref[rows, :] = xn
            else:
                xn = xn_ref[rows, :]
            gate = _dot(xn, wg_ref[...])
            up = _dot(xn, wu_ref[...])
            act = (gate * jax.nn.sigmoid(gate) * up).astype(BF16)
            for cols in chunks:
                part = _dot(act, wd_ref[:, cols])
                if first:
                    o_ref[rows, cols] = part
                else:
                    o_ref[rows, cols] += part
            if last:
                o_ref[rows, :] = x_ref[rows, :] + FFN_RES_SCALE * _rms(o_ref[rows, :], post_g_ref[...])

    pl.when(j == 0)(lambda: body(True, False))
    pl.when((j > 0) & (j < last_j))(lambda: body(False, False))
    pl.when(j == last_j)(lambda: body(False, True))


def _ffn(h, pre_g, w_gu, w_down, post_g, to_cast=(), *, tm, tf, sub):
    m, d = h.shape
    d_ff = w_down.shape[0]
    nf = d_ff // tf
    assert m % tm == 0 and tm % sub == 0 and d_ff % tf == 0 and d % tf == 0
    assert nf >= 2, "the first and the last d_ff step must be different steps"
    nc = nf - 1
    assert all(w.shape[0] % (m // tm * BF16_SUBLANES) == 0 and w.shape[1] % (nc * 128) == 0 for w in to_cast)
    blocks = [pl.BlockSpec((w.shape[0] // (m // tm), w.shape[1] // nc), lambda i, j: (i, jnp.minimum(j, nc - 1)))
              for w in to_cast]
    out = pl.pallas_call(
        functools.partial(_ffn_kernel, sub=sub, n_cast=len(to_cast)),
        grid=(m // tm, nf),
        in_specs=[
            pl.BlockSpec((tm, d), lambda i, j: (i, 0)),
            pl.BlockSpec((1, d), lambda i, j: (0, 0)),
            pl.BlockSpec((d, tf), lambda i, j: (0, j)),
            pl.BlockSpec((d, tf), lambda i, j: (0, j + nf)),
            pl.BlockSpec((tf, d), lambda i, j: (j, 0)),
            pl.BlockSpec((1, d), lambda i, j: (0, 0)),
        ] + blocks,
        out_specs=[pl.BlockSpec((tm, d), lambda i, j: (i, 0))] + blocks,
        out_shape=[jax.ShapeDtypeStruct((m, d), F32)] + [jax.ShapeDtypeStruct(w.shape, BF16) for w in to_cast],
        scratch_shapes=[pltpu.VMEM((tm, d), BF16)],
        compiler_params=_params("parallel", "arbitrary"),
        name="ffn",
    )(h, pre_g, w_gu, w_gu, w_down, post_g, *to_cast)
    return out[0], out[1:]


def _in_proj_kernel(x_ref, g_ref, w_ref, cos_ref, sin_ref, y_ref, q_ref, k_ref, v_ref, *, sub):
    width = y_ref.shape[1]
    n_heads = width // HEAD_DIM
    col = lambda n: w_ref[:, n * width:(n + 1) * width]

    for r in range(x_ref.shape[0] // sub):
        rows = slice(r * sub, (r + 1) * sub)
        xn = _rms(x_ref[rows, :], g_ref[...]).astype(BF16)
        cos, sin = cos_ref[rows, :], sin_ref[rows, :]

        def rope(z, o_ref):
            for h in range(n_heads):
                cols = slice(h * HEAD_DIM, (h + 1) * HEAD_DIM)
                zh = z[:, cols]
                o_ref[rows, cols] = zh * cos + pltpu.roll(zh, HEAD_DIM // 2, axis=1) * sin

        y_ref[rows, :] = _dot(xn, col(0)) * jax.nn.sigmoid(_dot(xn, col(1)))
        rope(_dot(xn, col(2)), q_ref)
        rope(_dot(xn, col(3)), k_ref)
        v_ref[rows, :] = _dot(xn, col(4)).astype(BF16)


def _in_proj(h, g, w_in, cos, sin, *, tm, seq, sub):
    m, d = h.shape
    width = w_in.shape[1] // 5
    assert seq % tm == 0 and tm % sub == 0 and width % HEAD_DIM == 0
    row = lambda i: (i, 0)
    table = pl.BlockSpec((tm, HEAD_DIM), lambda i: (i % (seq // tm), 0))
    return pl.pallas_call(
        functools.partial(_in_proj_kernel, sub=sub),
        grid=(m // tm,),
        in_specs=[pl.BlockSpec((tm, d), row), _resident((1, d)), _resident(w_in.shape), table, table],
        out_specs=[pl.BlockSpec((tm, width), row)] * 4,
        out_shape=[
            jax.ShapeDtypeStruct((m, width), F32),
            jax.ShapeDtypeStruct((m, width), F32),
            jax.ShapeDtypeStruct((m, width), F32),
            jax.ShapeDtypeStruct((m, width), BF16),
        ],
        compiler_params=_params("parallel"),
        name="in_proj",
    )(h, g, w_in, cos, sin)


def _conv_kernel(y_ref, w_ref, b_ref, ln_g_ref, ln_b_ref, o_ref, win_ref, shift_ref, w8_ref, conv_ref,
                 *, rows):
    t = pl.program_id(1)
    ts, c = y_ref.shape

    @pl.when(t == 0)
    def _():
        win_ref[0:CONV_PAD, :] = jnp.zeros((CONV_PAD, c), F32)
        for tap in range(CONV_KERNEL):
            w8_ref[tap] = jnp.broadcast_to(w_ref[tap:tap + 1, :], (SUBLANES, c))

    @pl.when(t > 0)
    def _():
        win_ref[0:CONV_PAD, :] = win_ref[ts:ts + CONV_PAD, :]

    win_ref[CONV_PAD:CONV_PAD + ts, :] = y_ref[...]
    span = shift_ref.shape[1]
    for b in range(1, SUBLANES):
        shift_ref[b - 1] = win_ref[b:b + span, :]

    first = CONV_PAD - (CONV_KERNEL - 1)
    n_sub = rows // SUBLANES

    def chunk(ci, carry):
        r0 = pl.multiple_of(ci * rows, rows)
        accs = [jnp.broadcast_to(b_ref[...], (SUBLANES, c))] * n_sub
        for tap in range(CONV_KERNEL):
            off, b = divmod(first + tap, SUBLANES)
            src = win_ref if b == 0 else shift_ref.at[b - 1]
            wt = w8_ref[tap]
            for i in range(n_sub):
                start = pl.multiple_of(r0 + (off + i) * SUBLANES, SUBLANES)
                accs[i] = accs[i] + src[pl.ds(start, SUBLANES), :] * wt
        for i in range(n_sub):
            conv_ref[pl.ds(pl.multiple_of(r0 + i * SUBLANES, SUBLANES), SUBLANES), :] = accs[i]
        return carry

    lax.fori_loop(0, ts // rows, chunk, 0, unroll=2)

    acc = conv_ref[...]
    mu = jnp.mean(acc, axis=-1, keepdims=True)
    xc = acc - mu
    var = jnp.mean(xc * xc, axis=-1, keepdims=True)
    yn = xc * lax.rsqrt(var + LN_EPS) * ln_g_ref[...] + ln_b_ref[...]
    o_ref[...] = (yn * jax.nn.sigmoid(yn)).astype(o_ref.dtype)


def _conv_group(y, w_dw, b_dw, ln_g, ln_b, *, ts, seq, rows):
    m, c = y.shape
    nt = seq // ts
    assert seq % ts == 0 and ts % (2 * rows) == 0 and rows % SUBLANES == 0 and CONV_PAD >= CONV_KERNEL - 1
    const = lambda b, t: (0, 0)
    return pl.pallas_call(
        functools.partial(_conv_kernel, rows=rows),
        grid=(m // seq, nt),
        in_specs=[
            pl.BlockSpec((ts, c), lambda b, t: (b * nt + t, 0)),
            pl.BlockSpec((CONV_KERNEL, c), const),
            pl.BlockSpec((1, c), const),
            pl.BlockSpec((1, c), const),
            pl.BlockSpec((1, c), const),
        ],
        out_specs=pl.BlockSpec((ts, c), lambda b, t: (b * nt + t, 0)),
        out_shape=jax.ShapeDtypeStruct((m, c), BF16),
        scratch_shapes=[
            pltpu.VMEM((CONV_PAD + ts, c), F32),
            pltpu.VMEM((SUBLANES - 1, CONV_PAD + ts - SUBLANES, c), F32),
            pltpu.VMEM((CONV_KERNEL, SUBLANES, c), F32),
            pltpu.VMEM((ts, c), F32),
        ],
        compiler_params=_params("parallel", "arbitrary"),
        name="conv_group",
    )(y, w_dw, b_dw, ln_g, ln_b)


def _moba_kernel(q_ref, k_ref, v_ref, *refs, n_cast):
    cast_in, o_ref, cast_out = refs[:n_cast], refs[n_cast], refs[n_cast + 1:2 * n_cast + 1]
    kb_ref, vt_ref, kmean_ref, p_ref = refs[2 * n_cast + 1:]
    for src, dst in zip(cast_in, cast_out):
        dst[...] = src[...].astype(dst.dtype)

    bs = MOBA_BLOCK
    nb = q_ref.shape[0] // bs
    n_heads = q_ref.shape[1] // HEAD_DIM
    c = HEAD_DIM ** -0.5 * math.log2(math.e)
    head = lambda h: slice(h * HEAD_DIM, (h + 1) * HEAD_DIM)

    for h in range(n_heads):
        for j in range(nb):
            blk_rows = slice(j * bs, (j + 1) * bs)
            kj = k_ref[blk_rows, head(h)]
            kb_ref[h, blk_rows, :] = kj.astype(BF16)
            kmean_ref[h, j:j + 1, :] = jnp.mean(kj, axis=0, keepdims=True)
            vt_ref[h, 0:HEAD_DIM, blk_rows] = v_ref[blk_rows, head(h)].astype(F32).T.astype(BF16)
        vt_ref[h, HEAD_DIM:, :] = jnp.ones((vt_ref.shape[1] - HEAD_DIM, vt_ref.shape[2]), BF16)

    blk = lax.broadcasted_iota(jnp.int32, (nb, bs), 0)
    causal = (lax.broadcasted_iota(jnp.int32, (bs, bs), 0)
              <= lax.broadcasted_iota(jnp.int32, (bs, bs), 1))

    def scores(h, qi):
        qt = q_ref[qi * bs:(qi + 1) * bs, head(h)].T
        s = _dot(kb_ref[h, 0:(qi + 1) * bs, :], (qt * c).astype(BF16))
        keep = None
        if qi > MOBA_TOPK:
            gate = jnp.dot(kmean_ref[h], qt, precision=lax.Precision.HIGHEST, preferred_element_type=F32)
            beaten_by = jnp.zeros((nb, bs), jnp.int32)
            for i in range(qi):
                gi = gate[i:i + 1, :]
                beats = (gi > gate) | ((gi == gate) & (i < blk))
                beaten_by = beaten_by + beats.astype(jnp.int32)
            keep = jnp.where((blk < qi) & (beaten_by < MOBA_TOPK), 1.0, 0.0)
        return s, keep

    def softmax(slot, qi, s, keep):
        blocks = []
        for j in range(qi + 1):
            sj = s[j * bs:(j + 1) * bs, :]
            if j == qi:
                sj = jnp.where(causal, sj, NEG_INF)
            elif keep is not None:
                sj = jnp.where(keep[j:j + 1, :] > 0.5, sj, NEG_INF)
            blocks.append(sj)
        m = functools.reduce(jnp.maximum, [jnp.max(sj, axis=0, keepdims=True) for sj in blocks])
        for j, sj in enumerate(blocks):
            p_ref[slot, j * bs:(j + 1) * bs, :] = jnp.exp2(sj - m).astype(BF16)

    def weighted_values(slot, h, qi):
        nk = (qi + 1) * bs
        acc = _dot(vt_ref[h, :, 0:nk], p_ref[slot, 0:nk, :])
        out = acc[0:HEAD_DIM, :] / acc[HEAD_DIM:HEAD_DIM + 1, :]
        o_ref[qi * bs:(qi + 1) * bs, head(h)] = out.T.astype(o_ref.dtype)

    items = [(h, qi) for qi in range(nb - 1, -1, -1) for h in range(n_heads)]
    scored = {}
    for step in range(len(items) + 2):
        if step < len(items):
            scored[step] = scores(*items[step])
        if 0 <= step - 1 < len(items):
            softmax((step - 1) % 2, items[step - 1][1], *scored.pop(step - 1))
        if 0 <= step - 2 < len(items):
            weighted_values((step - 2) % 2, *items[step - 2])


def _moba(q, k, v, to_cast, *, seq, heads_per_step):
    m, width = q.shape
    cols = heads_per_step * HEAD_DIM
    assert seq % MOBA_BLOCK == 0 and width % cols == 0
    groups = width // cols
    steps = (m // seq) * groups
    assert all(w.shape[0] % (steps * BF16_SUBLANES) == 0 for w in to_cast)
    spec = pl.BlockSpec((seq, cols), lambda b, h: (b, h))
    slabs = [pl.BlockSpec((w.shape[0] // steps, w.shape[1]), lambda b, h: (b * groups + h, 0)) for w in to_cast]
    out = pl.pallas_call(
        functools.partial(_moba_kernel, n_cast=len(to_cast)),
        grid=(m // seq, groups),
        in_specs=[spec, spec, spec] + slabs,
        out_specs=[spec] + slabs,
        out_shape=[jax.ShapeDtypeStruct((m, width), BF16)] + [jax.ShapeDtypeStruct(w.shape, BF16) for w in to_cast],
        scratch_shapes=[
            pltpu.VMEM((heads_per_step, seq, HEAD_DIM), BF16),
            pltpu.VMEM((heads_per_step, HEAD_DIM + BF16_SUBLANES, seq), BF16),
            pltpu.VMEM((heads_per_step, seq // MOBA_BLOCK, HEAD_DIM), F32),
            pltpu.VMEM((2, seq, MOBA_BLOCK), BF16),
        ],
        compiler_params=_params("parallel", "parallel"),
        name="moba",
    )(q, k, v, *to_cast)
    return out[0], out[1:]


def _mix_out_kernel(conv_ref, attn_ref, h_ref, w_out_ref, mix_g_ref, xpre_g_ref, wq_ref,
                    mem_ref, mem_g_ref, wkv_ref, wo_ref, xpost_g_ref, o_ref, kv_ref, *, sub, tiles_per_seq):
    @pl.when(pl.program_id(0) % tiles_per_seq == 0)
    def _():
        kv_ref[...] = _dot(_rms(mem_ref[...], mem_g_ref[...]).astype(BF16), wkv_ref[...]).astype(kv_ref.dtype)

    cw = conv_ref.shape[1]
    width = wq_ref.shape[1]
    scale = HEAD_DIM ** -0.5
    n_sub = h_ref.shape[0] // sub
    tile = lambda r: slice(r * sub, (r + 1) * sub)

    def mix(r):
        return _dot(conv_ref[tile(r), :], w_out_ref[0:cw, :]) + _dot(attn_ref[tile(r), :], w_out_ref[cw:, :])

    def query(r, y):
        h = h_ref[tile(r), :] + _rms(y, mix_g_ref[...])
        return h, _dot(_rms(h, xpre_g_ref[...]).astype(BF16), wq_ref[...])

    def attend(r, hq):
        h, q = hq
        heads = []
        for hd in range(N_XATTN_HEADS):
            cols = slice(hd * HEAD_DIM, (hd + 1) * HEAD_DIM)
            vcols = slice(width + hd * HEAD_DIM, width + (hd + 1) * HEAD_DIM)
            s = lax.dot_general(q[:, cols].astype(BF16), kv_ref[:, cols], _NT,
                                preferred_element_type=F32) * scale
            e = jnp.exp(s - jnp.max(s, axis=-1, keepdims=True))
            p = e / jnp.sum(e, axis=-1, keepdims=True)
            heads.append(_dot(p.astype(BF16), kv_ref[:, vcols]))
        return h, jnp.concatenate(heads, axis=1).astype(BF16)

    def project(r, ho):
        h, o = ho
        o_ref[tile(r), :] = h + _rms(_dot(o, wo_ref[...]), xpost_g_ref[...])

    stages = [lambda r, _: mix(r), query, attend, project]
    carried = {}
    for step in range(n_sub + len(stages) - 1):
        for k, stage in enumerate(stages):
            r = step - k
            if 0 <= r < n_sub:
                carried[r] = stage(r, carried.get(r))


def _mix_out(conv, attn, h, w_out, mix_g, xpre_g, w_q, mem, mem_g, w_kv, w_o, xpost_g, *, tm, seq, sub):
    m, d = h.shape
    mem_len = mem.shape[0] // (m // seq)
    assert seq % tm == 0 and tm % sub == 0 and w_out.shape[0] == conv.shape[1] + attn.shape[1]
    row = lambda i: (i, 0)
    return pl.pallas_call(
        functools.partial(_mix_out_kernel, sub=sub, tiles_per_seq=seq // tm),
        grid=(m // tm,),
        in_specs=[
            pl.BlockSpec((tm, conv.shape[1]), row),
            pl.BlockSpec((tm, attn.shape[1]), row),
            pl.BlockSpec((tm, d), row),
            _resident(w_out.shape),
            _resident((1, d)),
            _resident((1, d)),
            _resident(w_q.shape),
            pl.BlockSpec((mem_len, d), lambda i: (i // (seq // tm), 0)),
            _resident((1, d)),
            _resident(w_kv.shape),
            _resident(w_o.shape),
            _resident((1, d)),
        ],
        out_specs=pl.BlockSpec((tm, d), row),
        out_shape=jax.ShapeDtypeStruct((m, d), F32),
        scratch_shapes=[pltpu.VMEM((mem_len, w_kv.shape[1]), BF16)],
        compiler_params=_params("arbitrary"),
        name="mix_out",
    )(conv, attn, h, w_out, mix_g, xpre_g, w_q, mem, mem_g, w_kv, w_o, xpost_g)


def _rope_tables(seq):
    inv_freq = 1.0 / (ROPE_THETA ** (jnp.arange(0, HEAD_DIM, 2, dtype=F32) / HEAD_DIM))
    ang = jnp.arange(seq, dtype=F32)[:, None] * inv_freq[None, :]
    cos, sin = jnp.cos(ang), jnp.sin(ang)
    return jnp.concatenate([cos, cos], axis=-1), jnp.concatenate([-sin, sin], axis=-1)


def kernel(x, mem, ffn1_pre_g, ffn1_w_gu, ffn1_w_down, ffn1_post_g, mix_pre_g, w_in, conv_w_dw, conv_b_dw, conv_ln_g, conv_ln_b, w_out, mix_post_g, xattn_pre_g, mem_g, xattn_w_q, xattn_w_kv, xattn_w_o, xattn_post_g, ffn2_pre_g, ffn2_w_gu, ffn2_w_down, ffn2_post_g):
    batch, seq, d = x.shape
    mem_len = mem.shape[1]
    cos, sin = _rope_tables(seq)
    row = lambda a, l: a[l][None, :]
    h = x.reshape(batch * seq, d)
    mem2d = mem.reshape(batch * mem_len, d)

    for l in range(ffn1_w_gu.shape[0]):
        h, (w_in_b,) = _ffn(h, row(ffn1_pre_g, l), ffn1_w_gu[l].astype(BF16), ffn1_w_down[l].astype(BF16),
                            row(ffn1_post_g, l), (w_in[l],), tm=FFN_ROWS, tf=FFN_COLS, sub=FFN_SUB_ROWS)

        y, q, k, v = _in_proj(h, row(mix_pre_g, l), w_in_b, cos, sin,
                              tm=IN_PROJ_ROWS, seq=seq, sub=IN_PROJ_SUB_ROWS)
        conv = _conv_group(y, conv_w_dw[l], row(conv_b_dw, l), row(conv_ln_g, l), row(conv_ln_b, l),
                           ts=CONV_ROWS, seq=seq, rows=CONV_CHUNK_ROWS)
        attn, (w_out_b, w_q_b, w_kv_b, w_o_b, ffn2_w_gu_b, ffn2_w_down_b) = _moba(
            q, k, v, (w_out[l], xattn_w_q[l], xattn_w_kv[l], xattn_w_o[l], ffn2_w_gu[l], ffn2_w_down[l]),
            seq=seq, heads_per_step=MOBA_HEADS_PER_STEP)

        h = _mix_out(conv, attn, h, w_out_b, row(mix_post_g, l), row(xattn_pre_g, l),
                     w_q_b, mem2d, row(mem_g, l), w_kv_b, w_o_b, row(xattn_post_g, l),
                     tm=MIX_OUT_ROWS, seq=seq, sub=MIX_OUT_SUB_ROWS)

        h, _ = _ffn(h, row(ffn2_pre_g, l), ffn2_w_gu_b, ffn2_w_down_b,
                    row(ffn2_post_g, l), tm=FFN_ROWS, tf=FFN_COLS, sub=FFN_SUB_ROWS)
    return h.reshape(batch, seq, d)
```

```python
import functools
import math

import jax
import jax.numpy as jnp
from jax import lax
from jax.experimental import pallas as pl
from jax.experimental.pallas import tpu as pltpu

F32 = jnp.float32
BF16 = jnp.bfloat16

HEAD_DIM = 128
CONV_KERNEL = 31
MOBA_BLOCK = 256
MOBA_TOPK = 3
N_XATTN_HEADS = 4
ROPE_THETA = 10000.0
RMS_EPS = 1e-6
LN_EPS = 1e-5
FFN_RES_SCALE = 0.5
NEG_INF = -1e30

V7X_VMEM_BYTES = 64 * 1024 * 1024
VMEM_LIMIT_BYTES = V7X_VMEM_BYTES - 3 * 1024 * 1024
SUBLANES = 8
BF16_SUBLANES = 16
CONV_PAD = 32

_NT = (((1,), (1,)), ((), ()))

FFN_ROWS, FFN_SUB_ROWS, FFN_COLS = 1024, 512, 512
IN_PROJ_ROWS, IN_PROJ_SUB_ROWS = 512, 256
CONV_ROWS, CONV_CHUNK_ROWS = 512, 32
MIX_OUT_ROWS, MIX_OUT_SUB_ROWS = 512, 256
MEM_KV_ROWS = 256
MOBA_HEADS_PER_STEP = 2


def _params(*semantics):
    return pltpu.CompilerParams(dimension_semantics=semantics, vmem_limit_bytes=VMEM_LIMIT_BYTES)


def _resident(shape):
    return pl.BlockSpec(shape, lambda *_: (0,) * len(shape), pipeline_mode=pl.Buffered(1))


def _rms(x, g):
    return x * lax.rsqrt(jnp.mean(x * x, axis=-1, keepdims=True) + RMS_EPS) * g


def _dot(a, b):
    return jnp.dot(a, b, preferred_element_type=F32)


def _ffn_kernel(x_ref, pre_g_ref, wg_ref, wu_ref, wd_ref, post_g_ref, *refs, sub, n_cast):
    cast_in, o_ref, cast_out, xn_ref = refs[:n_cast], refs[n_cast], refs[n_cast + 1:2 * n_cast + 1], refs[-1]
    j = pl.program_id(1)
    last_j = pl.num_programs(1) - 1
    tiles = [slice(r * sub, (r + 1) * sub) for r in range(o_ref.shape[0] // sub)]
    tf = wd_ref.shape[0]
    chunks = [slice(n * tf, (n + 1) * tf) for n in range(o_ref.shape[1] // tf)]

    def body(first, last):
        for src, dst in zip(cast_in, cast_out):
            dst[...] = src[...].astype(dst.dtype)
        for rows in tiles:
            if first:
                xn = _rms(x_ref[rows, :], pre_g_ref[...]).astype(BF16)
                xn_ref[rows, :] = xn
            else:
                xn = xn_ref[rows, :]
            gate = _dot(xn, wg_ref[...])
            up = _dot(xn, wu_ref[...])
            act = (gate * jax.nn.sigmoid(gate) * up).astype(BF16)
            for cols in chunks:
                part = _dot(act, wd_ref[:, cols].astype(BF16))
                if first:
                    o_ref[rows, cols] = part
                else:
                    o_ref[rows, cols] += part
            if last:
                o_ref[rows, :] = x_ref[rows, :] + FFN_RES_SCALE * _rms(o_ref[rows, :], post_g_ref[...])

    pl.when(j == 0)(lambda: body(True, False))
    pl.when((j > 0) & (j < last_j))(lambda: body(False, False))
    pl.when(j == last_j)(lambda: body(False, True))


def _ffn(h, pre_g, w_gu, w_down, post_g, to_cast=(), *, tm, tf, sub):
    m, d = h.shape
    d_ff = w_down.shape[0]
    nf = d_ff // tf
    assert m % tm == 0 and tm % sub == 0 and d_ff % tf == 0 and d % tf == 0
    assert nf >= 2, "the first and the last d_ff step must be different steps"
    nc = nf - 1
    assert all(w.shape[0] % (m // tm * BF16_SUBLANES) == 0 and w.shape[1] % (nc * 128) == 0 for w in to_cast)
    blocks = [pl.BlockSpec((w.shape[0] // (m // tm), w.shape[1] // nc), lambda i, j: (i, jnp.minimum(j, nc - 1)))
              for w in to_cast]
    out = pl.pallas_call(
        functools.partial(_ffn_kernel, sub=sub, n_cast=len(to_cast)),
        grid=(m // tm, nf),
        in_specs=[
            pl.BlockSpec((tm, d), lambda i, j: (i, 0)),
            pl.BlockSpec((1, d), lambda i, j: (0, 0)),
            pl.BlockSpec((d, tf), lambda i, j: (0, j)),
            pl.BlockSpec((d, tf), lambda i, j: (0, j + nf)),
            pl.BlockSpec((tf, d), lambda i, j: (j, 0)),
            pl.BlockSpec((1, d), lambda i, j: (0, 0)),
        ] + blocks,
        out_specs=[pl.BlockSpec((tm, d), lambda i, j: (i, 0))] + blocks,
        out_shape=[jax.ShapeDtypeStruct((m, d), F32)] + [jax.ShapeDtypeStruct(w.shape, BF16) for w in to_cast],
        scratch_shapes=[pltpu.VMEM((tm, d), BF16)],
        compiler_params=_params("parallel", "arbitrary"),
        name="ffn",
    )(h, pre_g, w_gu, w_gu, w_down, post_g, *to_cast)
    return out[0], out[1:]


def _in_proj_kernel(x_ref, g_ref, w_ref, cos_ref, sin_ref, y_ref, q_ref, k_ref, v_ref, *, sub):
    width = y_ref.shape[1]
    n_heads = width // HEAD_DIM
    col = lambda n: w_ref[:, n * width:(n + 1) * width]

    for r in range(x_ref.shape[0] // sub):
        rows = slice(r * sub, (r + 1) * sub)
        xn = _rms(x_ref[rows, :], g_ref[...]).astype(BF16)
        cos, sin = cos_ref[rows, :], sin_ref[rows, :]

        def rope(z, o_ref):
            for h in range(n_heads):
                cols = slice(h * HEAD_DIM, (h + 1) * HEAD_DIM)
                zh = z[:, cols]
                o_ref[rows, cols] = zh * cos + pltpu.roll(zh, HEAD_DIM // 2, axis=1) * sin

        y_ref[rows, :] = _dot(xn, col(0)) * jax.nn.sigmoid(_dot(xn, col(1)))
        rope(_dot(xn, col(2)), q_ref)
        rope(_dot(xn, col(3)), k_ref)
        v_ref[rows, :] = _dot(xn, col(4)).astype(BF16)


def _in_proj(h, g, w_in, cos, sin, *, tm, seq, sub):
    m, d = h.shape
    width = w_in.shape[1] // 5
    assert seq % tm == 0 and tm % sub == 0 and width % HEAD_DIM == 0
    row = lambda i: (i, 0)
    table = pl.BlockSpec((tm, HEAD_DIM), lambda i: (i % (seq // tm), 0))
    return pl.pallas_call(
        functools.partial(_in_proj_kernel, sub=sub),
        grid=(m // tm,),
        in_specs=[pl.BlockSpec((tm, d), row), _resident((1, d)), _resident(w_in.shape), table, table],
        out_specs=[pl.BlockSpec((tm, width), row)] * 4,
        out_shape=[
            jax.ShapeDtypeStruct((m, width), F32),
            jax.ShapeDtypeStruct((m, width), F32),
            jax.ShapeDtypeStruct((m, width), F32),
            jax.ShapeDtypeStruct((m, width), BF16),
        ],
        compiler_params=_params("parallel"),
        name="in_proj",
    )(h, g, w_in, cos, sin)


def _conv_kernel(y_ref, w_ref, b_ref, ln_g_ref, ln_b_ref, o_ref, win_ref, shift_ref, w8_ref, conv_ref,
                 *, rows):
    t = pl.program_id(1)
    ts, c = y_ref.shape

    @pl.when(t == 0)
    def _():
        win_ref[0:CONV_PAD, :] = jnp.zeros((CONV_PAD, c), F32)
        for tap in range(CONV_KERNEL):
            w8_ref[tap] = jnp.broadcast_to(w_ref[tap:tap + 1, :], (SUBLANES, c))

    @pl.when(t > 0)
    def _():
        win_ref[0:CONV_PAD, :] = win_ref[ts:ts + CONV_PAD, :]

    win_ref[CONV_PAD:CONV_PAD + ts, :] = y_ref[...]
    span = shift_ref.shape[1]
    for b in range(1, SUBLANES):
        shift_ref[b - 1] = win_ref[b:b + span, :]

    first = CONV_PAD - (CONV_KERNEL - 1)
    n_sub = rows // SUBLANES

    def chunk(ci, carry):
        r0 = pl.multiple_of(ci * rows, rows)
        accs = [jnp.broadcast_to(b_ref[...], (SUBLANES, c))] * n_sub
        for tap in range(CONV_KERNEL):
            off, b = divmod(first + tap, SUBLANES)
            src = win_ref if b == 0 else shift_ref.at[b - 1]
            wt = w8_ref[tap]
            for i in range(n_sub):
                start = pl.multiple_of(r0 + (off + i) * SUBLANES, SUBLANES)
                accs[i] = accs[i] + src[pl.ds(start, SUBLANES), :] * wt
        for i in range(n_sub):
            conv_ref[pl.ds(pl.multiple_of(r0 + i * SUBLANES, SUBLANES), SUBLANES), :] = accs[i]
        return carry

    lax.fori_loop(0, ts // rows, chunk, 0, unroll=2)

    acc = conv_ref[...]
    mu = jnp.mean(acc, axis=-1, keepdims=True)
    xc = acc - mu
    var = jnp.mean(xc * xc, axis=-1, keepdims=True)
    yn = xc * lax.rsqrt(var + LN_EPS) * ln_g_ref[...] + ln_b_ref[...]
    o_ref[...] = (yn * jax.nn.sigmoid(yn)).astype(o_ref.dtype)


def _conv_group(y, w_dw, b_dw, ln_g, ln_b, *, ts, seq, rows):
    m, c = y.shape
    nt = seq // ts
    assert seq % ts == 0 and ts % (2 * rows) == 0 and rows % SUBLANES == 0 and CONV_PAD >= CONV_KERNEL - 1
    const = lambda b, t: (0, 0)
    return pl.pallas_call(
        functools.partial(_conv_kernel, rows=rows),
        grid=(m // seq, nt),
        in_specs=[
            pl.BlockSpec((ts, c), lambda b, t: (b * nt + t, 0)),
            pl.BlockSpec((CONV_KERNEL, c), const),
            pl.BlockSpec((1, c), const),
            pl.BlockSpec((1, c), const),
            pl.BlockSpec((1, c), const),
        ],
        out_specs=pl.BlockSpec((ts, c), lambda b, t: (b * nt + t, 0)),
        out_shape=jax.ShapeDtypeStruct((m, c), BF16),
        scratch_shapes=[
            pltpu.VMEM((CONV_PAD + ts, c), F32),
            pltpu.VMEM((SUBLANES - 1, CONV_PAD + ts - SUBLANES, c), F32),
            pltpu.VMEM((CONV_KERNEL, SUBLANES, c), F32),
            pltpu.VMEM((ts, c), F32),
        ],
        compiler_params=_params("parallel", "arbitrary"),
        name="conv_group",
    )(y, w_dw, b_dw, ln_g, ln_b)


def _moba_kernel(q_ref, k_ref, v_ref, *refs, n_cast):
    cast_in, o_ref, cast_out = refs[:n_cast], refs[n_cast], refs[n_cast + 1:2 * n_cast + 1]
    kb_ref, vt_ref, kmean_ref, p_ref = refs[2 * n_cast + 1:]
    for src, dst in zip(cast_in, cast_out):
        dst[...] = src[...].astype(dst.dtype)

    bs = MOBA_BLOCK
    nb = q_ref.shape[0] // bs
    n_heads = q_ref.shape[1] // HEAD_DIM
    c = HEAD_DIM ** -0.5 * math.log2(math.e)
    head = lambda h: slice(h * HEAD_DIM, (h + 1) * HEAD_DIM)

    for h in range(n_heads):
        for j in range(nb):
            blk_rows = slice(j * bs, (j + 1) * bs)
            kj = k_ref[blk_rows, head(h)]
            kb_ref[h, blk_rows, :] = kj.astype(BF16)
            kmean_ref[h, j:j + 1, :] = jnp.mean(kj, axis=0, keepdims=True)
            vt_ref[h, 0:HEAD_DIM, blk_rows] = v_ref[blk_rows, head(h)].astype(F32).T.astype(BF16)
        vt_ref[h, HEAD_DIM:, :] = jnp.ones((vt_ref.shape[1] - HEAD_DIM, vt_ref.shape[2]), BF16)

    blk = lax.broadcasted_iota(jnp.int32, (nb, bs), 0)
    causal = (lax.broadcasted_iota(jnp.int32, (bs, bs), 0)
              <= lax.broadcasted_iota(jnp.int32, (bs, bs), 1))

    def scores(h, qi):
        qt = q_ref[qi * bs:(qi + 1) * bs, head(h)].T
        s = _dot(kb_ref[h, 0:(qi + 1) * bs, :], (qt * c).astype(BF16))
        keep = None
        if qi > MOBA_TOPK:
            gate = jnp.dot(kmean_ref[h], qt, precision=lax.Precision.HIGHEST, preferred_element_type=F32)
            beaten_by = jnp.zeros((nb, bs), jnp.int32)
            for i in range(qi):
                gi = gate[i:i + 1, :]
                beats = (gi > gate) | ((gi == gate) & (i < blk))
                beaten_by = beaten_by + beats.astype(jnp.int32)
            keep = jnp.where((blk < qi) & (beaten_by < MOBA_TOPK), 1.0, 0.0)
        return s, keep

    def softmax(slot, qi, s, keep):
        blocks = []
        for j in range(qi + 1):
            sj = s[j * bs:(j + 1) * bs, :]
            if j == qi:
                sj = jnp.where(causal, sj, NEG_INF)
            elif keep is not None:
                sj = jnp.where(keep[j:j + 1, :] > 0.5, sj, NEG_INF)
            blocks.append(sj)
        m = functools.reduce(jnp.maximum, [jnp.max(sj, axis=0, keepdims=True) for sj in blocks])
        for j, sj in enumerate(blocks):
            p_ref[slot, j * bs:(j + 1) * bs, :] = jnp.exp2(sj - m).astype(BF16)

    def weighted_values(slot, h, qi):
        nk = (qi + 1) * bs
        acc = _dot(vt_ref[h, :, 0:nk], p_ref[slot, 0:nk, :])
        out = acc[0:HEAD_DIM, :] / acc[HEAD_DIM:HEAD_DIM + 1, :]
        o_ref[qi * bs:(qi + 1) * bs, head(h)] = out.T.astype(o_ref.dtype)

    items = [(h, qi) for qi in range(nb - 1, -1, -1) for h in range(n_heads)]
    scored = {}
    for step in range(len(items) + 2):
        if step < len(items):
            scored[step] = scores(*items[step])
        if 0 <= step - 1 < len(items):
            softmax((step - 1) % 2, items[step - 1][1], *scored.pop(step - 1))
        if 0 <= step - 2 < len(items):
            weighted_values((step - 2) % 2, *items[step - 2])


def _moba(q, k, v, to_cast, *, seq, heads_per_step):
    m, width = q.shape
    cols = heads_per_step * HEAD_DIM
    assert seq % MOBA_BLOCK == 0 and width % cols == 0
    groups = width // cols
    steps = (m // seq) * groups
    assert all(w.shape[0] % (steps * BF16_SUBLANES) == 0 for w in to_cast)
    spec = pl.BlockSpec((seq, cols), lambda b, h: (b, h))
    slabs = [pl.BlockSpec((w.shape[0] // steps, w.shape[1]), lambda b, h: (b * groups + h, 0)) for w in to_cast]
    out = pl.pallas_call(
        functools.partial(_moba_kernel, n_cast=len(to_cast)),
        grid=(m // seq, groups),
        in_specs=[spec, spec, spec] + slabs,
        out_specs=[spec] + slabs,
        out_shape=[jax.ShapeDtypeStruct((m, width), BF16)] + [jax.ShapeDtypeStruct(w.shape, BF16) for w in to_cast],
        scratch_shapes=[
            pltpu.VMEM((heads_per_step, seq, HEAD_DIM), BF16),
            pltpu.VMEM((heads_per_step, HEAD_DIM + BF16_SUBLANES, seq), BF16),
            pltpu.VMEM((heads_per_step, seq // MOBA_BLOCK, HEAD_DIM), F32),
            pltpu.VMEM((2, seq, MOBA_BLOCK), BF16),
        ],
        compiler_params=_params("parallel", "parallel"),
        name="moba",
    )(q, k, v, *to_cast)
    return out[0], out[1:]


def _norm_matmul_kernel(x_ref, g_ref, w_ref, o_ref):
    xn = _rms(x_ref[...], g_ref[...]).astype(BF16)
    o_ref[...] = _dot(xn, w_ref[...]).astype(o_ref.dtype)


def _norm_matmul(x, g, w, *, tm, out_dtype):
    m, d = x.shape
    n = w.shape[1]
    return pl.pallas_call(
        _norm_matmul_kernel,
        grid=(m // tm,),
        in_specs=[pl.BlockSpec((tm, d), lambda i: (i, 0)), _resident((1, d)), _resident((d, n))],
        out_specs=pl.BlockSpec((tm, n), lambda i: (i, 0)),
        out_shape=jax.ShapeDtypeStruct((m, n), out_dtype),
        compiler_params=_params("parallel"),
        name="mem_kv",
    )(x, g, w)


def _mix_out_kernel(conv_ref, attn_ref, h_ref, w_out_ref, mix_g_ref,
                    xpre_g_ref, wq_ref, kv_ref, wo_ref, xpost_g_ref, o_ref, *, sub):
    cw = conv_ref.shape[1]
    width = wq_ref.shape[1]
    scale = HEAD_DIM ** -0.5
    n_sub = h_ref.shape[0] // sub
    tile = lambda r: slice(r * sub, (r + 1) * sub)

    def mix(r):
        return _dot(conv_ref[tile(r), :], w_out_ref[0:cw, :]) + _dot(attn_ref[tile(r), :], w_out_ref[cw:, :])

    def query(r, y):
        h = h_ref[tile(r), :] + _rms(y, mix_g_ref[...])
        return h, _dot(_rms(h, xpre_g_ref[...]).astype(BF16), wq_ref[...])

    def attend(r, hq):
        h, q = hq
        heads = []
        for hd in range(N_XATTN_HEADS):
            cols = slice(hd * HEAD_DIM, (hd + 1) * HEAD_DIM)
            vcols = slice(width + hd * HEAD_DIM, width + (hd + 1) * HEAD_DIM)
            s = lax.dot_general(q[:, cols].astype(BF16), kv_ref[:, cols], _NT,
                                preferred_element_type=F32) * scale
            e = jnp.exp(s - jnp.max(s, axis=-1, keepdims=True))
            p = e / jnp.sum(e, axis=-1, keepdims=True)
            heads.append(_dot(p.astype(BF16), kv_ref[:, vcols]))
        return h, jnp.concatenate(heads, axis=1).astype(BF16)

    def project(r, ho):
        h, o = ho
        o_ref[tile(r), :] = h + _rms(_dot(o, wo_ref[...]), xpost_g_ref[...])

    stages = [lambda r, _: mix(r), query, attend, project]
    carried = {}
    for step in range(n_sub + len(stages) - 1):
        for k, stage in enumerate(stages):
            r = step - k
            if 0 <= r < n_sub:
                carried[r] = stage(r, carried.get(r))


def _mix_out(conv, attn, h, w_out, mix_g, xpre_g, w_q, kv, w_o, xpost_g, *, tm, seq, sub):
    m, d = h.shape
    mem_len = kv.shape[0] // (m // seq)
    assert seq % tm == 0 and tm % sub == 0 and w_out.shape[0] == conv.shape[1] + attn.shape[1]
    row = lambda i: (i, 0)
    return pl.pallas_call(
        functools.partial(_mix_out_kernel, sub=sub),
        grid=(m // tm,),
        in_specs=[
            pl.BlockSpec((tm, conv.shape[1]), row),
            pl.BlockSpec((tm, attn.shape[1]), row),
            pl.BlockSpec((tm, d), row),
            _resident(w_out.shape),
            _resident((1, d)),
            _resident((1, d)),
            _resident(w_q.shape),
            pl.BlockSpec((mem_len, kv.shape[1]), lambda i: (i // (seq // tm), 0)),
            _resident(w_o.shape),
            _resident((1, d)),
        ],
        out_specs=pl.BlockSpec((tm, d), row),
        out_shape=jax.ShapeDtypeStruct((m, d), F32),
        compiler_params=_params("parallel"),
        name="mix_out",
    )(conv, attn, h, w_out, mix_g, xpre_g, w_q, kv, w_o, xpost_g)


def _rope_tables(seq):
    inv_freq = 1.0 / (ROPE_THETA ** (jnp.arange(0, HEAD_DIM, 2, dtype=F32) / HEAD_DIM))
    ang = jnp.arange(seq, dtype=F32)[:, None] * inv_freq[None, :]
    cos, sin = jnp.cos(ang), jnp.sin(ang)
    return jnp.concatenate([cos, cos], axis=-1), jnp.concatenate([-sin, sin], axis=-1)


def kernel(x, mem, ffn1_pre_g, ffn1_w_gu, ffn1_w_down, ffn1_post_g, mix_pre_g, w_in, conv_w_dw, conv_b_dw, conv_ln_g, conv_ln_b, w_out, mix_post_g, xattn_pre_g, mem_g, xattn_w_q, xattn_w_kv, xattn_w_o, xattn_post_g, ffn2_pre_g, ffn2_w_gu, ffn2_w_down, ffn2_post_g):
    batch, seq, d = x.shape
    mem_len = mem.shape[1]
    cos, sin = _rope_tables(seq)
    row = lambda a, l: a[l][None, :]
    h = x.reshape(batch * seq, d)
    mem2d = mem.reshape(batch * mem_len, d)

    for l in range(ffn1_w_gu.shape[0]):
        h, (w_in_b,) = _ffn(h, row(ffn1_pre_g, l), ffn1_w_gu[l].astype(BF16), ffn1_w_down[l],
                            row(ffn1_post_g, l), (w_in[l],), tm=FFN_ROWS, tf=FFN_COLS, sub=FFN_SUB_ROWS)

        y, q, k, v = _in_proj(h, row(mix_pre_g, l), w_in_b, cos, sin,
                              tm=IN_PROJ_ROWS, seq=seq, sub=IN_PROJ_SUB_ROWS)
        conv = _conv_group(y, conv_w_dw[l], row(conv_b_dw, l), row(conv_ln_g, l), row(conv_ln_b, l),
                           ts=CONV_ROWS, seq=seq, rows=CONV_CHUNK_ROWS)
        attn, (w_out_b, w_q_b, w_kv_b, w_o_b, ffn2_w_gu_b, ffn2_w_down_b) = _moba(
            q, k, v, (w_out[l], xattn_w_q[l], xattn_w_kv[l], xattn_w_o[l], ffn2_w_gu[l], ffn2_w_down[l]),
            seq=seq, heads_per_step=MOBA_HEADS_PER_STEP)

        kv = _norm_matmul(mem2d, row(mem_g, l), w_kv_b, tm=MEM_KV_ROWS, out_dtype=BF16)
        h = _mix_out(conv, attn, h, w_out_b, row(mix_post_g, l), row(xattn_pre_g, l),
                     w_q_b, kv, w_o_b, row(xattn_post_g, l),
                     tm=MIX_OUT_ROWS, seq=seq, sub=MIX_OUT_SUB_ROWS)

        h, _ = _ffn(h, row(ffn2_pre_g, l), ffn2_w_gu_b, ffn2_w_down_b,
                    row(ffn2_post_g, l), tm=FFN_ROWS, tf=FFN_COLS, sub=FFN_SUB_ROWS)
    return h.reshape(batch, seq, d)
```

```python
import functools
import math

import jax
import jax.numpy as jnp
from jax import lax
from jax.experimental import pallas as pl
from jax.experimental.pallas import tpu as pltpu

F32 = jnp.float32
BF16 = jnp.bfloat16

HEAD_DIM = 128
CONV_KERNEL = 31
MOBA_BLOCK = 256
MOBA_TOPK = 3
N_XATTN_HEADS = 4
ROPE_THETA = 10000.0
RMS_EPS = 1e-6
LN_EPS = 1e-5
FFN_RES_SCALE = 0.5
NEG_INF = -1e30

V7X_VMEM_BYTES = 64 * 1024 * 1024
VMEM_LIMIT_BYTES = V7X_VMEM_BYTES - 3 * 1024 * 1024
SUBLANES = 8
BF16_SUBLANES = 16
CONV_PAD = 32

_NT = (((1,), (1,)), ((), ()))

FFN_ROWS, FFN_SUB_ROWS, FFN_COLS = 1024, 512, 512
IN_PROJ_ROWS, IN_PROJ_SUB_ROWS = 512, 256
CONV_ROWS, CONV_CHUNK_ROWS = 512, 32
MIX_OUT_ROWS, MIX_OUT_SUB_ROWS = 512, 256
MEM_KV_ROWS = 256
MOBA_HEADS_PER_STEP = 4


def _params(*semantics):
    return pltpu.CompilerParams(dimension_semantics=semantics, vmem_limit_bytes=VMEM_LIMIT_BYTES)


def _resident(shape):
    return pl.BlockSpec(shape, lambda *_: (0,) * len(shape), pipeline_mode=pl.Buffered(1))


def _rms(x, g):
    return x * lax.rsqrt(jnp.mean(x * x, axis=-1, keepdims=True) + RMS_EPS) * g


def _dot(a, b):
    return jnp.dot(a, b, preferred_element_type=F32)


def _ffn_kernel(x_ref, pre_g_ref, wg_ref, wu_ref, wd_ref, post_g_ref, *refs, sub, n_cast):
    cast_in, o_ref, cast_out, xn_ref = refs[:n_cast], refs[n_cast], refs[n_cast + 1:2 * n_cast + 1], refs[-1]
    j = pl.program_id(1)
    last_j = pl.num_programs(1) - 1
    tiles = [slice(r * sub, (r + 1) * sub) for r in range(o_ref.shape[0] // sub)]
    tf = wd_ref.shape[0]
    chunks = [slice(n * tf, (n + 1) * tf) for n in range(o_ref.shape[1] // tf)]

    def body(first, last):
        for src, dst in zip(cast_in, cast_out):
            dst[...] = src[...].astype(dst.dtype)
        for rows in tiles:
            if first:
                xn = _rms(x_ref[rows, :], pre_g_ref[...]).astype(BF16)
                xn_ref[rows, :] = xn
            else:
                xn = xn_ref[rows, :]
            gate = _dot(xn, wg_ref[...])
            up = _dot(xn, wu_ref[...])
            act = (gate * jax.nn.sigmoid(gate) * up).astype(BF16)
            for cols in chunks:
                part = _dot(act, wd_ref[:, cols].astype(BF16))
                if first:
                    o_ref[rows, cols] = part
                else:
                    o_ref[rows, cols] += part
            if last:
                o_ref[rows, :] = x_ref[rows, :] + FFN_RES_SCALE * _rms(o_ref[rows, :], post_g_ref[...])

    pl.when(j == 0)(lambda: body(True, False))
    pl.when((j > 0) & (j < last_j))(lambda: body(False, False))
    pl.when(j == last_j)(lambda: body(False, True))


def _ffn(h, pre_g, w_gu, w_down, post_g, to_cast=(), *, tm, tf, sub):
    m, d = h.shape
    d_ff = w_down.shape[0]
    nf = d_ff // tf
    assert m % tm == 0 and tm % sub == 0 and d_ff % tf == 0 and d % tf == 0
    assert nf >= 2, "the first and the last d_ff step must be different steps"
    nc = nf - 1
    assert all(w.shape[0] % (m // tm * BF16_SUBLANES) == 0 and w.shape[1] % (nc * 128) == 0 for w in to_cast)
    blocks = [pl.BlockSpec((w.shape[0] // (m // tm), w.shape[1] // nc), lambda i, j: (i, jnp.minimum(j, nc - 1)))
              for w in to_cast]
    out = pl.pallas_call(
        functools.partial(_ffn_kernel, sub=sub, n_cast=len(to_cast)),
        grid=(m // tm, nf),
        in_specs=[
            pl.BlockSpec((tm, d), lambda i, j: (i, 0)),
            pl.BlockSpec((1, d), lambda i, j: (0, 0)),
            pl.BlockSpec((d, tf), lambda i, j: (0, j)),
            pl.BlockSpec((d, tf), lambda i, j: (0, j + nf)),
            pl.BlockSpec((tf, d), lambda i, j: (j, 0)),
            pl.BlockSpec((1, d), lambda i, j: (0, 0)),
        ] + blocks,
        out_specs=[pl.BlockSpec((tm, d), lambda i, j: (i, 0))] + blocks,
        out_shape=[jax.ShapeDtypeStruct((m, d), F32)] + [jax.ShapeDtypeStruct(w.shape, BF16) for w in to_cast],
        scratch_shapes=[pltpu.VMEM((tm, d), BF16)],
        compiler_params=_params("parallel", "arbitrary"),
        name="ffn",
    )(h, pre_g, w_gu, w_gu, w_down, post_g, *to_cast)
    return out[0], out[1:]


def _in_proj_kernel(x_ref, g_ref, w_ref, cos_ref, sin_ref, y_ref, q_ref, k_ref, v_ref, *, sub):
    width = y_ref.shape[1]
    n_heads = width // HEAD_DIM
    col = lambda n: w_ref[:, n * width:(n + 1) * width]

    for r in range(x_ref.shape[0] // sub):
        rows = slice(r * sub, (r + 1) * sub)
        xn = _rms(x_ref[rows, :], g_ref[...]).astype(BF16)
        cos, sin = cos_ref[rows, :], sin_ref[rows, :]

        def rope(z, o_ref):
            for h in range(n_heads):
                cols = slice(h * HEAD_DIM, (h + 1) * HEAD_DIM)
                zh = z[:, cols]
                o_ref[rows, cols] = zh * cos + pltpu.roll(zh, HEAD_DIM // 2, axis=1) * sin

        y_ref[rows, :] = _dot(xn, col(0)) * jax.nn.sigmoid(_dot(xn, col(1)))
        rope(_dot(xn, col(2)), q_ref)
        rope(_dot(xn, col(3)), k_ref)
        v_ref[rows, :] = _dot(xn, col(4)).astype(BF16)


def _in_proj(h, g, w_in, cos, sin, *, tm, seq, sub):
    m, d = h.shape
    width = w_in.shape[1] // 5
    assert seq % tm == 0 and tm % sub == 0 and width % HEAD_DIM == 0
    row = lambda i: (i, 0)
    table = pl.BlockSpec((tm, HEAD_DIM), lambda i: (i % (seq // tm), 0))
    return pl.pallas_call(
        functools.partial(_in_proj_kernel, sub=sub),
        grid=(m // tm,),
        in_specs=[pl.BlockSpec((tm, d), row), _resident((1, d)), _resident(w_in.shape), table, table],
        out_specs=[pl.BlockSpec((tm, width), row)] * 4,
        out_shape=[
            jax.ShapeDtypeStruct((m, width), F32),
            jax.ShapeDtypeStruct((m, width), F32),
            jax.ShapeDtypeStruct((m, width), F32),
            jax.ShapeDtypeStruct((m, width), BF16),
        ],
        compiler_params=_params("parallel"),
        name="in_proj",
    )(h, g, w_in, cos, sin)


def _conv_kernel(y_ref, w_ref, b_ref, ln_g_ref, ln_b_ref, *refs, rows, n_cast):
    cast_in, o_ref, cast_out = refs[:n_cast], refs[n_cast], refs[n_cast + 1:2 * n_cast + 1]
    win_ref, shift_ref, w8_ref, conv_ref = refs[2 * n_cast + 1:]
    for src, dst in zip(cast_in, cast_out):
        dst[...] = src[...].astype(dst.dtype)

    t = pl.program_id(1)
    ts, c = y_ref.shape

    @pl.when(t == 0)
    def _():
        win_ref[0:CONV_PAD, :] = jnp.zeros((CONV_PAD, c), F32)
        for tap in range(CONV_KERNEL):
            w8_ref[tap] = jnp.broadcast_to(w_ref[tap:tap + 1, :], (SUBLANES, c))

    @pl.when(t > 0)
    def _():
        win_ref[0:CONV_PAD, :] = win_ref[ts:ts + CONV_PAD, :]

    win_ref[CONV_PAD:CONV_PAD + ts, :] = y_ref[...]
    span = shift_ref.shape[1]
    for b in range(1, SUBLANES):
        shift_ref[b - 1] = win_ref[b:b + span, :]

    first = CONV_PAD - (CONV_KERNEL - 1)
    n_sub = rows // SUBLANES

    def chunk(ci, carry):
        r0 = pl.multiple_of(ci * rows, rows)
        accs = [jnp.broadcast_to(b_ref[...], (SUBLANES, c))] * n_sub
        for tap in range(CONV_KERNEL):
            off, b = divmod(first + tap, SUBLANES)
            src = win_ref if b == 0 else shift_ref.at[b - 1]
            wt = w8_ref[tap]
            for i in range(n_sub):
                start = pl.multiple_of(r0 + (off + i) * SUBLANES, SUBLANES)
                accs[i] = accs[i] + src[pl.ds(start, SUBLANES), :] * wt
        for i in range(n_sub):
            conv_ref[pl.ds(pl.multiple_of(r0 + i * SUBLANES, SUBLANES), SUBLANES), :] = accs[i]
        return carry

    lax.fori_loop(0, ts // rows, chunk, 0, unroll=2)

    acc = conv_ref[...]
    mu = jnp.mean(acc, axis=-1, keepdims=True)
    xc = acc - mu
    var = jnp.mean(xc * xc, axis=-1, keepdims=True)
    yn = xc * lax.rsqrt(var + LN_EPS) * ln_g_ref[...] + ln_b_ref[...]
    o_ref[...] = (yn * jax.nn.sigmoid(yn)).astype(o_ref.dtype)


def _conv_group(y, w_dw, b_dw, ln_g, ln_b, to_cast=(), *, ts, seq, rows):
    m, c = y.shape
    nt = seq // ts
    assert seq % ts == 0 and ts % (2 * rows) == 0 and rows % SUBLANES == 0 and CONV_PAD >= CONV_KERNEL - 1
    steps = (m // seq) * nt
    assert all(w.shape[0] % (steps * BF16_SUBLANES) == 0 for w in to_cast)
    const = lambda b, t: (0, 0)
    slabs = [pl.BlockSpec((w.shape[0] // steps, w.shape[1]), lambda b, t: (b * nt + t, 0)) for w in to_cast]
    out = pl.pallas_call(
        functools.partial(_conv_kernel, rows=rows, n_cast=len(to_cast)),
        grid=(m // seq, nt),
        in_specs=[
            pl.BlockSpec((ts, c), lambda b, t: (b * nt + t, 0)),
            pl.BlockSpec((CONV_KERNEL, c), const),
            pl.BlockSpec((1, c), const),
            pl.BlockSpec((1, c), const),
            pl.BlockSpec((1, c), const),
        ] + slabs,
        out_specs=[pl.BlockSpec((ts, c), lambda b, t: (b * nt + t, 0))] + slabs,
        out_shape=[jax.ShapeDtypeStruct((m, c), BF16)] + [jax.ShapeDtypeStruct(w.shape, BF16) for w in to_cast],
        scratch_shapes=[
            pltpu.VMEM((CONV_PAD + ts, c), F32),
            pltpu.VMEM((SUBLANES - 1, CONV_PAD + ts - SUBLANES, c), F32),
            pltpu.VMEM((CONV_KERNEL, SUBLANES, c), F32),
            pltpu.VMEM((ts, c), F32),
        ],
        compiler_params=_params("parallel", "arbitrary"),
        name="conv_group",
    )(y, w_dw, b_dw, ln_g, ln_b, *to_cast)
    return out[0], out[1:]


def _moba_kernel(q_ref, k_ref, v_ref, *refs, n_cast):
    cast_in, o_ref, cast_out = refs[:n_cast], refs[n_cast], refs[n_cast + 1:2 * n_cast + 1]
    kb_ref, vt_ref, kmean_ref, p_ref = refs[2 * n_cast + 1:]
    for src, dst in zip(cast_in, cast_out):
        dst[...] = src[...].astype(dst.dtype)

    bs = MOBA_BLOCK
    nb = q_ref.shape[0] // bs
    n_heads = q_ref.shape[1] // HEAD_DIM
    c = HEAD_DIM ** -0.5 * math.log2(math.e)
    head = lambda h: slice(h * HEAD_DIM, (h + 1) * HEAD_DIM)

    for h in range(n_heads):
        for j in range(nb):
            blk_rows = slice(j * bs, (j + 1) * bs)
            kj = k_ref[blk_rows, head(h)]
            kb_ref[h, blk_rows, :] = kj.astype(BF16)
            kmean_ref[h, j:j + 1, :] = jnp.mean(kj, axis=0, keepdims=True)
            vt_ref[h, 0:HEAD_DIM, blk_rows] = v_ref[blk_rows, head(h)].astype(F32).T.astype(BF16)
        vt_ref[h, HEAD_DIM:, :] = jnp.ones((vt_ref.shape[1] - HEAD_DIM, vt_ref.shape[2]), BF16)

    blk = lax.broadcasted_iota(jnp.int32, (nb, bs), 0)
    causal = (lax.broadcasted_iota(jnp.int32, (bs, bs), 0)
              <= lax.broadcasted_iota(jnp.int32, (bs, bs), 1))

    def scores(h, qi):
        qt = q_ref[qi * bs:(qi + 1) * bs, head(h)].T
        s = _dot(kb_ref[h, 0:(qi + 1) * bs, :], (qt * c).astype(BF16))
        keep = None
        if qi > MOBA_TOPK:
            gate = jnp.dot(kmean_ref[h], qt, precision=lax.Precision.HIGHEST, preferred_element_type=F32)
            beaten_by = jnp.zeros((nb, bs), jnp.int32)
            for i in range(qi):
                gi = gate[i:i + 1, :]
                beats = (gi > gate) | ((gi == gate) & (i < blk))
                beaten_by = beaten_by + beats.astype(jnp.int32)
            keep = jnp.where((blk < qi) & (beaten_by < MOBA_TOPK), 1.0, 0.0)
        return s, keep

    def softmax(slot, qi, s, keep):
        blocks = []
        for j in range(qi + 1):
            sj = s[j * bs:(j + 1) * bs, :]
            if j == qi:
                sj = jnp.where(causal, sj, NEG_INF)
            elif keep is not None:
                sj = jnp.where(keep[j:j + 1, :] > 0.5, sj, NEG_INF)
            blocks.append(sj)
        m = functools.reduce(jnp.maximum, [jnp.max(sj, axis=0, keepdims=True) for sj in blocks])
        for j, sj in enumerate(blocks):
            p_ref[slot, j * bs:(j + 1) * bs, :] = jnp.exp2(sj - m).astype(BF16)

    def weighted_values(slot, h, qi):
        nk = (qi + 1) * bs
        acc = _dot(vt_ref[h, :, 0:nk], p_ref[slot, 0:nk, :])
        out = acc[0:HEAD_DIM, :] / acc[HEAD_DIM:HEAD_DIM + 1, :]
        o_ref[qi * bs:(qi + 1) * bs, head(h)] = out.T.astype(o_ref.dtype)

    items = [(h, qi) for qi in range(nb - 1, -1, -1) for h in range(n_heads)]
    scored = {}
    for step in range(len(items) + 2):
        if step < len(items):
            scored[step] = scores(*items[step])
        if 0 <= step - 1 < len(items):
            softmax((step - 1) % 2, items[step - 1][1], *scored.pop(step - 1))
        if 0 <= step - 2 < len(items):
            weighted_values((step - 2) % 2, *items[step - 2])


def _moba(q, k, v, to_cast, *, seq, heads_per_step):
    m, width = q.shape
    cols = heads_per_step * HEAD_DIM
    assert seq % MOBA_BLOCK == 0 and width % cols == 0
    groups = width // cols
    steps = (m // seq) * groups
    assert all(w.shape[0] % (steps * BF16_SUBLANES) == 0 for w in to_cast)
    spec = pl.BlockSpec((seq, cols), lambda b, h: (b, h))
    slabs = [pl.BlockSpec((w.shape[0] // steps, w.shape[1]), lambda b, h: (b * groups + h, 0)) for w in to_cast]
    out = pl.pallas_call(
        functools.partial(_moba_kernel, n_cast=len(to_cast)),
        grid=(m // seq, groups),
        in_specs=[spec, spec, spec] + slabs,
        out_specs=[spec] + slabs,
        out_shape=[jax.ShapeDtypeStruct((m, width), BF16)] + [jax.ShapeDtypeStruct(w.shape, BF16) for w in to_cast],
        scratch_shapes=[
            pltpu.VMEM((heads_per_step, seq, HEAD_DIM), BF16),
            pltpu.VMEM((heads_per_step, HEAD_DIM + BF16_SUBLANES, seq), BF16),
            pltpu.VMEM((heads_per_step, seq // MOBA_BLOCK, HEAD_DIM), F32),
            pltpu.VMEM((2, seq, MOBA_BLOCK), BF16),
        ],
        compiler_params=_params("parallel", "parallel"),
        name="moba",
    )(q, k, v, *to_cast)
    return out[0], out[1:]


def _norm_matmul_kernel(x_ref, g_ref, w_ref, o_ref):
    xn = _rms(x_ref[...], g_ref[...]).astype(BF16)
    o_ref[...] = _dot(xn, w_ref[...]).astype(o_ref.dtype)


def _norm_matmul(x, g, w, *, tm, out_dtype):
    m, d = x.shape
    n = w.shape[1]
    return pl.pallas_call(
        _norm_matmul_kernel,
        grid=(m // tm,),
        in_specs=[pl.BlockSpec((tm, d), lambda i: (i, 0)), _resident((1, d)), _resident((d, n))],
        out_specs=pl.BlockSpec((tm, n), lambda i: (i, 0)),
        out_shape=jax.ShapeDtypeStruct((m, n), out_dtype),
        compiler_params=_params("parallel"),
        name="mem_kv",
    )(x, g, w)


def _mix_out_kernel(conv_ref, attn_ref, h_ref, w_out_ref, mix_g_ref,
                    xpre_g_ref, wq_ref, kv_ref, wo_ref, xpost_g_ref, o_ref, *, sub):
    cw = conv_ref.shape[1]
    width = wq_ref.shape[1]
    scale = HEAD_DIM ** -0.5
    n_sub = h_ref.shape[0] // sub
    tile = lambda r: slice(r * sub, (r + 1) * sub)

    def mix(r):
        return _dot(conv_ref[tile(r), :], w_out_ref[0:cw, :]) + _dot(attn_ref[tile(r), :], w_out_ref[cw:, :])

    def query(r, y):
        h = h_ref[tile(r), :] + _rms(y, mix_g_ref[...])
        return h, _dot(_rms(h, xpre_g_ref[...]).astype(BF16), wq_ref[...])

    def attend(r, hq):
        h, q = hq
        heads = []
        for hd in range(N_XATTN_HEADS):
            cols = slice(hd * HEAD_DIM, (hd + 1) * HEAD_DIM)
            vcols = slice(width + hd * HEAD_DIM, width + (hd + 1) * HEAD_DIM)
            s = lax.dot_general(q[:, cols].astype(BF16), kv_ref[:, cols], _NT,
                                preferred_element_type=F32) * scale
            e = jnp.exp(s - jnp.max(s, axis=-1, keepdims=True))
            p = e / jnp.sum(e, axis=-1, keepdims=True)
            heads.append(_dot(p.astype(BF16), kv_ref[:, vcols]))
        return h, jnp.concatenate(heads, axis=1).astype(BF16)

    def project(r, ho):
        h, o = ho
        o_ref[tile(r), :] = h + _rms(_dot(o, wo_ref[...]), xpost_g_ref[...])

    stages = [lambda r, _: mix(r), query, attend, project]
    carried = {}
    for step in range(n_sub + len(stages) - 1):
        for k, stage in enumerate(stages):
            r = step - k
            if 0 <= r < n_sub:
                carried[r] = stage(r, carried.get(r))


def _mix_out(conv, attn, h, w_out, mix_g, xpre_g, w_q, kv, w_o, xpost_g, *, tm, seq, sub):
    m, d = h.shape
    mem_len = kv.shape[0] // (m // seq)
    assert seq % tm == 0 and tm % sub == 0 and w_out.shape[0] == conv.shape[1] + attn.shape[1]
    row = lambda i: (i, 0)
    return pl.pallas_call(
        functools.partial(_mix_out_kernel, sub=sub),
        grid=(m // tm,),
        in_specs=[
            pl.BlockSpec((tm, conv.shape[1]), row),
            pl.BlockSpec((tm, attn.shape[1]), row),
            pl.BlockSpec((tm, d), row),
            _resident(w_out.shape),
            _resident((1, d)),
            _resident((1, d)),
            _resident(w_q.shape),
            pl.BlockSpec((mem_len, kv.shape[1]), lambda i: (i // (seq // tm), 0)),
            _resident(w_o.shape),
            _resident((1, d)),
        ],
        out_specs=pl.BlockSpec((tm, d), row),
        out_shape=jax.ShapeDtypeStruct((m, d), F32),
        compiler_params=_params("parallel"),
        name="mix_out",
    )(conv, attn, h, w_out, mix_g, xpre_g, w_q, kv, w_o, xpost_g)


def _rope_tables(seq):
    inv_freq = 1.0 / (ROPE_THETA ** (jnp.arange(0, HEAD_DIM, 2, dtype=F32) / HEAD_DIM))
    ang = jnp.arange(seq, dtype=F32)[:, None] * inv_freq[None, :]
    cos, sin = jnp.cos(ang), jnp.sin(ang)
    return jnp.concatenate([cos, cos], axis=-1), jnp.concatenate([-sin, sin], axis=-1)


def kernel(x, mem, ffn1_pre_g, ffn1_w_gu, ffn1_w_down, ffn1_post_g, mix_pre_g, w_in, conv_w_dw, conv_b_dw, conv_ln_g, conv_ln_b, w_out, mix_post_g, xattn_pre_g, mem_g, xattn_w_q, xattn_w_kv, xattn_w_o, xattn_post_g, ffn2_pre_g, ffn2_w_gu, ffn2_w_down, ffn2_post_g):
    batch, seq, d = x.shape
    mem_len = mem.shape[1]
    cos, sin = _rope_tables(seq)
    row = lambda a, l: a[l][None, :]
    h = x.reshape(batch * seq, d)
    mem2d = mem.reshape(batch * mem_len, d)

    for l in range(ffn1_w_gu.shape[0]):
        h, (w_in_b,) = _ffn(h, row(ffn1_pre_g, l), ffn1_w_gu[l].astype(BF16), ffn1_w_down[l],
                            row(ffn1_post_g, l), (w_in[l],), tm=FFN_ROWS, tf=FFN_COLS, sub=FFN_SUB_ROWS)

        y, q, k, v = _in_proj(h, row(mix_pre_g, l), w_in_b, cos, sin,
                              tm=IN_PROJ_ROWS, seq=seq, sub=IN_PROJ_SUB_ROWS)
        conv, (w_out_b, w_q_b, w_kv_b, w_o_b, ffn2_w_gu_b, ffn2_w_down_b) = _conv_group(
            y, conv_w_dw[l], row(conv_b_dw, l), row(conv_ln_g, l), row(conv_ln_b, l),
            (w_out[l], xattn_w_q[l], xattn_w_kv[l], xattn_w_o[l], ffn2_w_gu[l], ffn2_w_down[l]),
            ts=CONV_ROWS, seq=seq, rows=CONV_CHUNK_ROWS)
        attn, _ = _moba(q, k, v, (), seq=seq, heads_per_step=MOBA_HEADS_PER_STEP)

        kv = _norm_matmul(mem2d, row(mem_g, l), w_kv_b, tm=MEM_KV_ROWS, out_dtype=BF16)
        h = _mix_out(conv, attn, h, w_out_b, row(mix_post_g, l), row(xattn_pre_g, l),
                     w_q_b, kv, w_o_b, row(xattn_post_g, l),
                     tm=MIX_OUT_ROWS, seq=seq, sub=MIX_OUT_SUB_ROWS)

        h, _ = _ffn(h, row(ffn2_pre_g, l), ffn2_w_gu_b, ffn2_w_down_b,
                    row(ffn2_post_g, l), tm=FFN_ROWS, tf=FFN_COLS, sub=FFN_SUB_ROWS)
    return h.reshape(batch, seq, d)
```
